```python
import jax, jax.numpy as jnp
from jax import lax
import numpy as np

D_MODEL = 2048
BATCH = 4
SEQ = 4096
DEPTH = 4

N_MIXERS = 2
N_HEADS = 16
N_KV_HEADS = 4
HEAD_DIM = D_MODEL // N_HEADS
GROUP = N_HEADS // N_KV_HEADS
ROPE_DIM = HEAD_DIM // 4
ROPE_THETA = 500000.0
IDX_HEADS = 16
IDX_DIM = 64
IDX_ROPE_DIM = IDX_DIM // 4
TOPK_MAX = 256
Q_BLOCK = 128
Q_W = N_HEADS * HEAD_DIM
KV_W = N_KV_HEADS * HEAD_DIM
IDX_Q_W = IDX_HEADS * IDX_DIM
ATTN_PROJ = Q_W + 2 * KV_W + IDX_Q_W + IDX_DIM + IDX_HEADS
CONV_WIDTH = 3
D_FF = 4 * D_MODEL
EPS = 1e-6
N_ATTN_LAYERS = (DEPTH + 1) // 2
N_CONV_LAYERS = DEPTH // 2
MAX_POS_OFFSET = 1024

kernel_name = "hybrid_dsa_shortconv_sqrelu_trunk"


def _rmsnorm(x, g):
    xf = x.astype(jnp.float32)
    y = xf * lax.rsqrt(jnp.mean(xf * xf, axis=-1, keepdims=True) + EPS)
    return (y * g.astype(jnp.float32)).astype(x.dtype)


def _rope_tables(positions, rot_dim, dtype):
    inv = ROPE_THETA ** (-jnp.arange(0, rot_dim, 2, dtype=jnp.float32) / rot_dim)
    ang = positions.astype(jnp.float32)[..., None] * inv
    return jnp.cos(ang)[:, :, None, :].astype(dtype), jnp.sin(ang)[:, :, None, :].astype(dtype)


def _partial_rope(x, cos, sin):
    half = cos.shape[-1]
    x1 = x[..., :half]
    x2 = x[..., half:2 * half]
    return jnp.concatenate([x1 * cos - x2 * sin, x2 * cos + x1 * sin, x[..., 2 * half:]], axis=-1)


def _dsa_mixer(h, w_in, q_g, k_g, w_out, positions):
    B, S, _ = h.shape
    proj = h @ w_in
    cuts = [Q_W, Q_W + KV_W, Q_W + 2 * KV_W, Q_W + 2 * KV_W + IDX_Q_W,
            Q_W + 2 * KV_W + IDX_Q_W + IDX_DIM]
    q, k, v, qi, ki, wi = jnp.split(proj, cuts, axis=-1)
    q = _rmsnorm(q.reshape(B, S, N_HEADS, HEAD_DIM), q_g)
    k = _rmsnorm(k.reshape(B, S, N_KV_HEADS, HEAD_DIM), k_g)
    v = v.reshape(B, S, N_KV_HEADS, HEAD_DIM)
    cos, sin = _rope_tables(positions, ROPE_DIM, h.dtype)
    q = _partial_rope(q, cos, sin)
    k = _partial_rope(k, cos, sin)
    ci, si = _rope_tables(positions, IDX_ROPE_DIM, h.dtype)
    qi = _partial_rope(qi.reshape(B, S, IDX_HEADS, IDX_DIM), ci, si)
    ki = _partial_rope(ki.reshape(B, S, 1, IDX_DIM), ci, si)[:, :, 0]
    wi = wi * (IDX_HEADS ** -0.5 * IDX_DIM ** -0.5)

    top_k = min(TOPK_MAX, S // 4)
    qb = min(Q_BLOCK, S)
    nb = S // qb
    scale = HEAD_DIM ** -0.5
    key_pos = jnp.arange(S)

    def to_blocks(a):
        return a.reshape((B, nb, qb) + a.shape[2:]).swapaxes(0, 1)

    def block(args):
        q_blk, qi_blk, wi_blk, b_idx = args
        t = b_idx * qb + jnp.arange(qb)
        rel = jax.nn.relu(jnp.einsum('bqhd,bsd->bqhs', qi_blk, ki)).astype(jnp.float32)
        score = jnp.einsum('bqhs,bqh->bqs', rel, wi_blk.astype(jnp.float32))
        causal = key_pos[None, :] <= t[:, None]
        score = jnp.where(causal[None], score, -jnp.inf)
        _, idx = lax.top_k(score, top_k)
        valid = idx <= t[None, :, None]
        k_sel = jax.vmap(lambda kb, ib: kb[ib])(k, idx)
        v_sel = jax.vmap(lambda vb, ib: vb[ib])(v, idx)
        qg = q_blk.reshape(B, qb, N_KV_HEADS, GROUP, HEAD_DIM)
        s = jnp.einsum('bqngd,bqknd->bqngk', qg, k_sel).astype(jnp.float32) * scale
        s = jnp.where(valid[:, :, None, None, :], s, -jnp.inf)
        p = jax.nn.softmax(s, axis=-1).astype(v.dtype)
        o = jnp.einsum('bqngk,bqknd->bqngd', p, v_sel)
        return o.reshape(B, qb, Q_W)

    out = lax.map(block, (to_blocks(q), to_blocks(qi), to_blocks(wi), jnp.arange(nb)))
    out = out.swapaxes(0, 1).reshape(B, S, Q_W)
    return out @ w_out


def _short_conv_mixer(h, w_in, conv_w, w_out):
    S = h.shape[1]
    b_gate, c_gate, u = jnp.split(h @ w_in, 3, axis=-1)
    z = c_gate * u
    zp = jnp.pad(z, ((0, 0), (CONV_WIDTH - 1, 0), (0, 0)))
    y = sum(zp[:, j:j + S] * conv_w[j] for j in range(CONV_WIDTH))
    return (b_gate * y) @ w_out


def _sqrelu_mlp(h, w1, w2):
    a = jax.nn.relu(h @ w1)
    return (a * a) @ w2


def setup_inputs(seed: int = 0) -> dict:
    key = jax.random.key(seed)
    ks = jax.random.split(key, 16)
    f32 = jnp.float32

    def nrm(k, shape, fan_in):
        return jax.random.normal(k, shape, f32) * (fan_in ** -0.5)

    def gain(k, shape):
        return 1.0 + 0.02 * jax.random.normal(k, shape, f32)

    x = jax.random.normal(ks[0], (BATCH, SEQ, D_MODEL), f32)
    offs = jax.random.randint(ks[1], (BATCH, 1), 0, MAX_POS_OFFSET, dtype=jnp.int32)
    positions = offs + jnp.arange(SEQ, dtype=jnp.int32)[None, :]
    return {
        "x": x,
        "positions": positions,
        "attn_norm_g": gain(ks[2], (N_ATTN_LAYERS, D_MODEL)),
        "attn_w_in": nrm(ks[3], (N_ATTN_LAYERS, D_MODEL, ATTN_PROJ), D_MODEL),
        "attn_q_norm_g": gain(ks[4], (N_ATTN_LAYERS, HEAD_DIM)),
        "attn_k_norm_g": gain(ks[5], (N_ATTN_LAYERS, HEAD_DIM)),
        "attn_w_out": nrm(ks[6], (N_ATTN_LAYERS, Q_W, D_MODEL), Q_W),
        "conv_norm_g": gain(ks[7], (N_CONV_LAYERS, D_MODEL)),
        "conv_w_in": nrm(ks[8], (N_CONV_LAYERS, D_MODEL, 3 * D_MODEL), D_MODEL),
        "conv_w": nrm(ks[9], (N_CONV_LAYERS, CONV_WIDTH, D_MODEL), CONV_WIDTH),
        "conv_w_out": nrm(ks[10], (N_CONV_LAYERS, D_MODEL, D_MODEL), D_MODEL),
        "mlp_norm_g": gain(ks[11], (DEPTH, D_MODEL)),
        "mlp_w1": nrm(ks[12], (DEPTH, D_MODEL, D_FF), D_MODEL),
        "mlp_w2": nrm(ks[13], (DEPTH, D_FF, D_MODEL), D_FF),
    }


def reference(x, positions, attn_norm_g, attn_w_in, attn_q_norm_g, attn_k_norm_g, attn_w_out,
              conv_norm_g, conv_w_in, conv_w, conv_w_out, mlp_norm_g, mlp_w1, mlp_w2):
    for i in range(DEPTH):
        j = i // N_MIXERS
        if i % N_MIXERS == 0:
            x = x + _dsa_mixer(_rmsnorm(x, attn_norm_g[j]), attn_w_in[j], attn_q_norm_g[j],
                               attn_k_norm_g[j], attn_w_out[j], positions)
        else:
            x = x + _short_conv_mixer(_rmsnorm(x, conv_norm_g[j]), conv_w_in[j], conv_w[j],
                                      conv_w_out[j])
        x = x + _sqrelu_mlp(_rmsnorm(x, mlp_norm_g[i]), mlp_w1[i], mlp_w2[i])
    return x
```

```python
import functools

import jax
import jax.numpy as jnp
from jax import lax
from jax.experimental import pallas as pl
from jax.experimental.pallas import tpu as pltpu

D_MODEL = 2048
N_HEADS = 16
N_KV_HEADS = 4
HEAD_DIM = 128
GROUP = N_HEADS // N_KV_HEADS
ROPE_DIM = HEAD_DIM // 4
ROPE_THETA = 500000.0
IDX_HEADS = 16
IDX_DIM = 64
IDX_ROPE_DIM = IDX_DIM // 4
TOPK_MAX = 256
Q_W = N_HEADS * HEAD_DIM
KV_W = N_KV_HEADS * HEAD_DIM
IDX_Q_W = IDX_HEADS * IDX_DIM
MAIN_W = Q_W + 2 * KV_W + IDX_Q_W
D_FF = 4 * D_MODEL
EPS = 1e-6
LANES = 128
SUBLANES = 8
VMEM_LIMIT = 56 * 1024 * 1024

INT_MIN = -(2 ** 31)
MASKED = -1e30

F32 = jnp.float32
BF16 = jnp.bfloat16


def _rmsnorm_rows(x, g):
    ms = jnp.mean(x * x, axis=-1, keepdims=True)
    return (x * lax.rsqrt(ms + EPS)) * g


def _rope_tables(pos, inv, period, half):
    ang = pos.astype(F32) * inv
    cos, sin = jnp.cos(ang), jnp.sin(ang)
    d = lax.broadcasted_iota(jnp.int32, ang.shape, 1) & (period - 1)
    c = jnp.where(d < 2 * half, cos, 1.0)
    s_lo = jnp.where(d < half, -sin, 0.0)
    s_hi = jnp.where((d >= half) & (d < 2 * half), sin, 0.0)
    return c, s_lo, s_hi


def _rope(x, c, s_lo, s_hi, half):
    return x * c + pltpu.roll(x, LANES - half, 1) * s_lo + pltpu.roll(x, half, 1) * s_hi


def _attn_in_kernel(x_ref, pos_ref, g_ref, w_ref, wkw_ref, qg_ref, kg_ref, invq_ref, invi_ref,
                    main_ref, kw_ref, h_ref, tq_ref, ti_ref, *, tn):
    j = pl.program_id(1)
    n_q = Q_W // tn
    n_k = KV_W // tn

    @pl.when(j == 0)
    def _():
        h = _rmsnorm_rows(x_ref[...], g_ref[...]).astype(BF16)
        h_ref[...] = h
        pos = pos_ref[...]
        cq, sq_lo, sq_hi = _rope_tables(pos, invq_ref[...], HEAD_DIM, ROPE_DIM // 2)
        tq_ref[0], tq_ref[1], tq_ref[2] = cq, sq_lo, sq_hi
        ci, si_lo, si_hi = _rope_tables(pos, invi_ref[...], IDX_DIM, IDX_ROPE_DIM // 2)
        ti_ref[0], ti_ref[1], ti_ref[2] = ci, si_lo, si_hi
        kw = jnp.dot(h, wkw_ref[...], preferred_element_type=F32)
        lane = lax.broadcasted_iota(jnp.int32, kw.shape, 1)
        roped = _rope(kw, ci, si_lo, si_hi, IDX_ROPE_DIM // 2)
        scale = IDX_HEADS ** -0.5 * IDX_DIM ** -0.5
        kw_ref[...] = jnp.where(lane < IDX_DIM, roped, kw * scale)

    acc = jnp.dot(h_ref[...], w_ref[...], preferred_element_type=F32)

    def heads_norm_rope(gain_ref):
        for hh in range(tn // HEAD_DIM):
            sl = slice(hh * HEAD_DIM, (hh + 1) * HEAD_DIM)
            y = _rmsnorm_rows(acc[:, sl], gain_ref[...])
            y = _rope(y, tq_ref[0], tq_ref[1], tq_ref[2], ROPE_DIM // 2)
            main_ref[:, sl] = y.astype(BF16)

    @pl.when(j < n_q)
    def _():
        heads_norm_rope(qg_ref)

    @pl.when((j >= n_q) & (j < n_q + n_k))
    def _():
        heads_norm_rope(kg_ref)

    @pl.when((j >= n_q + n_k) & (j < n_q + 2 * n_k))
    def _():
        main_ref[...] = acc.astype(BF16)

    @pl.when(j >= n_q + 2 * n_k)
    def _():
        for hh in range(tn // LANES):
            sl = slice(hh * LANES, (hh + 1) * LANES)
            y = _rope(acc[:, sl], ti_ref[0], ti_ref[1], ti_ref[2], IDX_ROPE_DIM // 2)
            main_ref[:, sl] = y.astype(BF16)


def _attn_in(x2d, pos2d, g, w_main, w_kw, q_g, k_g, inv_q, inv_i, *, tm=1024, tn=512):
    t = x2d.shape[0]
    grid = (t // tm, MAIN_W // tn)
    return pl.pallas_call(
        functools.partial(_attn_in_kernel, tn=tn),
        grid=grid,
        in_specs=[
            pl.BlockSpec((tm, D_MODEL), lambda i, j: (i, 0)),
            pl.BlockSpec((tm, 1), lambda i, j: (i, 0)),
            pl.BlockSpec((1, D_MODEL), lambda i, j: (0, 0)),
            pl.BlockSpec((D_MODEL, tn), lambda i, j: (0, j)),
            pl.BlockSpec((D_MODEL, LANES), lambda i, j: (0, 0)),
            pl.BlockSpec((1, HEAD_DIM), lambda i, j: (0, 0)),
            pl.BlockSpec((1, HEAD_DIM), lambda i, j: (0, 0)),
            pl.BlockSpec((1, LANES), lambda i, j: (0, 0)),
            pl.BlockSpec((1, LANES), lambda i, j: (0, 0)),
        ],
        out_specs=[
            pl.BlockSpec((tm, tn), lambda i, j: (i, j)),
            pl.BlockSpec((tm, LANES), lambda i, j: (i, 0)),
        ],
        out_shape=[
            jax.ShapeDtypeStruct((t, MAIN_W), BF16),
            jax.ShapeDtypeStruct((t, LANES), F32),
        ],
        scratch_shapes=[
            pltpu.VMEM((tm, D_MODEL), BF16),
            pltpu.VMEM((3, tm, LANES), F32),
            pltpu.VMEM((3, tm, LANES), F32),
        ],
        compiler_params=pltpu.CompilerParams(
            dimension_semantics=("arbitrary", "arbitrary"), vmem_limit_bytes=VMEM_LIMIT),
        name="attn_in",
    )(x2d, pos2d, g, w_main, w_kw, q_g, k_g, inv_q, inv_i)


def _dsa_kernel(qit_ref, wit_ref, ki_ref, qt_ref, k_ref, vt_ref, o_ref, key_ref, bias_ref,
                *, tq, kc, top_k):
    i = pl.program_id(1)
    t0 = i * tq
    n_kc = (t0 + tq + kc - 1) // kc
    row = lax.broadcasted_iota(jnp.int32, (kc, tq), 0)
    t_idx = t0 + lax.broadcasted_iota(jnp.int32, (kc, tq), 1)

    def score_chunk(c, carry):
        k_rows = ki_ref[0, pl.ds(pl.multiple_of(c * kc, kc), kc), :]
        acc = jnp.zeros((kc, tq), F32)
        for h in range(IDX_HEADS):
            rel = jnp.dot(k_rows, qit_ref[0, h * IDX_DIM:(h + 1) * IDX_DIM, :],
                          preferred_element_type=F32)
            acc = acc + wit_ref[0, h:h + 1, :] * jnp.maximum(rel, 0.0)
        bits = pltpu.bitcast(acc, jnp.int32)
        key = bits ^ ((bits >> 31) & 0x7FFFFFFF)
        key_ref[c] = jnp.where(c * kc + row <= t_idx, key, INT_MIN)
        return carry

    lax.fori_loop(0, n_kc, score_chunk, 0)

    def count(pred_fn):
        def body(c, cnt):
            m = pred_fn(c, key_ref[c]).astype(jnp.int32)
            return cnt + m.reshape(kc // SUBLANES, SUBLANES, tq).sum(axis=0)
        part = lax.fori_loop(0, n_kc, body, jnp.zeros((SUBLANES, tq), jnp.int32))
        return part.sum(axis=0, keepdims=True)

    lo = jnp.where(count(lambda c, kk: kk >= 0) >= top_k, 0, INT_MIN).astype(jnp.int32)

    def bit_body(b, lo):
        cand = lo + jnp.left_shift(jnp.int32(1), 30 - b)
        return jnp.where(count(lambda c, kk: kk >= cand) >= top_k, cand, lo)

    thr = lax.fori_loop(0, 31, bit_body, lo)
    thr_sel = jnp.maximum(thr, INT_MIN + 1)

    def bias_chunk(c, cnt):
        sel = key_ref[c] >= thr_sel
        bias_ref[c] = jnp.where(sel, 0.0, MASKED)
        return cnt + sel.astype(jnp.int32).reshape(kc // SUBLANES, SUBLANES, tq).sum(axis=0)

    n_sel = lax.fori_loop(0, n_kc, bias_chunk, jnp.zeros((SUBLANES, tq), jnp.int32))
    n_sel = n_sel.sum(axis=0, keepdims=True)

    @pl.when(jnp.max(n_sel) > top_k)
    def _():
        need = top_k - count(lambda c, kk: kk > thr)

        def cut_body(b, p):
            cand = p + jnp.left_shift(jnp.int32(1), 11 - b)
            below = count(lambda c, kk: (kk == thr) & (c * kc + row < cand))
            return jnp.where(below < need, cand, p)

        cut = lax.fori_loop(0, 12, cut_body, jnp.zeros((1, tq), jnp.int32))

        def rebias_chunk(c, carry):
            kk = key_ref[c]
            sel = (kk > thr) | ((kk == thr) & (c * kc + row <= cut))
            sel = sel & (kk != INT_MIN)
            bias_ref[c] = jnp.where(sel, 0.0, MASKED)
            return carry

        lax.fori_loop(0, n_kc, rebias_chunk, 0)

    scale = HEAD_DIM ** -0.5
    for h in range(N_HEADS):
        n = h // GROUP
        q_h = qt_ref[0, h * HEAD_DIM:(h + 1) * HEAD_DIM, :]

        def attn_chunk(c, carry, n=n, q_h=q_h):
            m, l, acc = carry
            k_c = k_ref[0, pl.ds(pl.multiple_of(c * kc, kc), kc), n * HEAD_DIM:(n + 1) * HEAD_DIM]
            s = jnp.dot(k_c, q_h, preferred_element_type=F32) * scale + bias_ref[c]
            m_new = jnp.maximum(m, jnp.max(s, axis=0, keepdims=True))
            p = jnp.exp(s - m_new)
            alpha = jnp.exp(m - m_new)
            l = alpha * l + jnp.sum(p, axis=0, keepdims=True)
            v_c = vt_ref[0, c, n * HEAD_DIM:(n + 1) * HEAD_DIM, :]
            acc = alpha * acc + jnp.dot(v_c, p.astype(BF16), preferred_element_type=F32)
            return m_new, l, acc

        init = (jnp.full((1, tq), MASKED, F32), jnp.zeros((1, tq), F32),
                jnp.zeros((HEAD_DIM, tq), F32))
        _, l, acc = lax.fori_loop(0, n_kc, attn_chunk, init)
        o_ref[0, :, h * HEAD_DIM:(h + 1) * HEAD_DIM] = (acc / l).T.astype(BF16)


def _dsa(qit, wit, ki, qt, main3, vt, *, tq=256, kc=512):
    b, s, _ = main3.shape
    top_k = min(TOPK_MAX, s // 4)
    n_chunks = s // kc
    k_col_block = Q_W // KV_W
    return pl.pallas_call(
        functools.partial(_dsa_kernel, tq=tq, kc=kc, top_k=top_k),
        grid=(b, s // tq),
        in_specs=[
            pl.BlockSpec((1, IDX_Q_W, tq), lambda bb, i: (bb, 0, i)),
            pl.BlockSpec((1, IDX_HEADS, tq), lambda bb, i: (bb, 0, i)),
            pl.BlockSpec((1, s, IDX_DIM), lambda bb, i: (bb, 0, 0)),
            pl.BlockSpec((1, Q_W, tq), lambda bb, i: (bb, 0, i)),
            pl.BlockSpec((1, s, KV_W), lambda bb, i: (bb, 0, k_col_block)),
            pl.BlockSpec((1, n_chunks, KV_W, kc), lambda bb, i: (bb, 0, 0, 0)),
        ],
        out_specs=pl.BlockSpec((1, tq, Q_W), lambda bb, i: (bb, i, 0)),
        out_shape=jax.ShapeDtypeStruct((b, s, Q_W), BF16),
        scratch_shapes=[
            pltpu.VMEM((n_chunks, kc, tq), jnp.int32),
            pltpu.VMEM((n_chunks, kc, tq), F32),
        ],
        compiler_params=pltpu.CompilerParams(
            dimension_semantics=("arbitrary", "arbitrary"), vmem_limit_bytes=VMEM_LIMIT),
        name="dsa",
    )(qit, wit, ki, qt, main3, vt)


def _out_proj_kernel(a_ref, w_ref, x_ref, o_ref):
    o_ref[...] = x_ref[...] + jnp.dot(a_ref[...], w_ref[...], preferred_element_type=F32)


def _out_proj(a2d, w, x2d, *, tm=1024, tn=512):
    t, k = a2d.shape
    n = w.shape[1]
    return pl.pallas_call(
        _out_proj_kernel,
        grid=(t // tm, n // tn),
        in_specs=[
            pl.BlockSpec((tm, k), lambda i, j: (i, 0)),
            pl.BlockSpec((k, tn), lambda i, j: (0, j)),
            pl.BlockSpec((tm, tn), lambda i, j: (i, j)),
        ],
        out_specs=pl.BlockSpec((tm, tn), lambda i, j: (i, j)),
        out_shape=jax.ShapeDtypeStruct((t, n), F32),
        compiler_params=pltpu.CompilerParams(
            dimension_semantics=("arbitrary", "arbitrary"), vmem_limit_bytes=VMEM_LIMIT),
        name="out_proj",
    )(a2d, w, x2d)


def _conv_kernel(x_ref, g_ref, wb_ref, wc_ref, wu_ref, cw_ref, wo_ref, o_ref, h_ref, carry_ref,
                 *, tm, tiles_per_seq):
    i = pl.program_id(0)
    j = pl.program_id(1)

    @pl.when(j == 0)
    def _():
        x = x_ref[...]
        h_ref[...] = _rmsnorm_rows(x, g_ref[...]).astype(BF16)
        o_ref[...] = x

    h = h_ref[...]
    b_gate = jnp.dot(h, wb_ref[...], preferred_element_type=F32)
    c_gate = jnp.dot(h, wc_ref[...], preferred_element_type=F32)
    u = jnp.dot(h, wu_ref[...], preferred_element_type=F32)
    z = c_gate * u
    prev = jnp.where(i % tiles_per_seq == 0, 0.0, carry_ref[j])
    carry_ref[j] = z[tm - SUBLANES:, :]
    r = lax.broadcasted_iota(jnp.int32, z.shape, 0)
    z1 = jnp.where(r == 0, prev[7:8, :], pltpu.roll(z, 1, 0))
    z2 = jnp.where(r == 0, prev[6:7, :], jnp.where(r == 1, prev[7:8, :], pltpu.roll(z, 2, 0)))
    y = z2 * cw_ref[0:1, :] + z1 * cw_ref[1:2, :] + z * cw_ref[2:3, :]
    gated = (b_gate * y).astype(BF16)
    o_ref[...] += jnp.dot(gated, wo_ref[...], preferred_element_type=F32)


def _conv_mixer(x2d, g, wb, wc, wu, cw, wo, *, seq, tm=512, tn=512):
    t = x2d.shape[0]
    n_chunks = D_MODEL // tn
    return pl.pallas_call(
        functools.partial(_conv_kernel, tm=tm, tiles_per_seq=seq // tm),
        grid=(t // tm, n_chunks),
        in_specs=[
            pl.BlockSpec((tm, D_MODEL), lambda i, j: (i, 0)),
            pl.BlockSpec((1, D_MODEL), lambda i, j: (0, 0)),
            pl.BlockSpec((D_MODEL, tn), lambda i, j: (0, j)),
            pl.BlockSpec((D_MODEL, tn), lambda i, j: (0, j)),
            pl.BlockSpec((D_MODEL, tn), lambda i, j: (0, j)),
            pl.BlockSpec((3, tn), lambda i, j: (0, j)),
            pl.BlockSpec((tn, D_MODEL), lambda i, j: (j, 0)),
        ],
        out_specs=pl.BlockSpec((tm, D_MODEL), lambda i, j: (i, 0)),
        out_shape=jax.ShapeDtypeStruct((t, D_MODEL), F32),
        scratch_shapes=[
            pltpu.VMEM((tm, D_MODEL), BF16),
            pltpu.VMEM((n_chunks, SUBLANES, tn), F32),
        ],
        compiler_params=pltpu.CompilerParams(
            dimension_semantics=("arbitrary", "arbitrary"), vmem_limit_bytes=VMEM_LIMIT),
        name="conv_mixer",
    )(x2d, g, wb, wc, wu, cw, wo)


def _mlp_kernel(x_ref, g_ref, w1_ref, w2_ref, o_ref, h_ref):
    @pl.when(pl.program_id(1) == 0)
    def _():
        x = x_ref[...]
        h_ref[...] = _rmsnorm_rows(x, g_ref[...]).astype(BF16)
        o_ref[...] = x

    a = jnp.maximum(jnp.dot(h_ref[...], w1_ref[...], preferred_element_type=F32), 0.0)
    o_ref[...] += jnp.dot((a * a).astype(BF16), w2_ref[...], preferred_element_type=F32)


def _mlp(x2d, g, w1, w2, *, tm=512, tf=512):
    t = x2d.shape[0]
    return pl.pallas_call(
        _mlp_kernel,
        grid=(t // tm, D_FF // tf),
        in_specs=[
            pl.BlockSpec((tm, D_MODEL), lambda i, j: (i, 0)),
            pl.BlockSpec((1, D_MODEL), lambda i, j: (0, 0)),
            pl.BlockSpec((D_MODEL, tf), lambda i, j: (0, j)),
            pl.BlockSpec((tf, D_MODEL), lambda i, j: (j, 0)),
        ],
        out_specs=pl.BlockSpec((tm, D_MODEL), lambda i, j: (i, 0)),
        out_shape=jax.ShapeDtypeStruct((t, D_MODEL), F32),
        scratch_shapes=[pltpu.VMEM((tm, D_MODEL), BF16)],
        compiler_params=pltpu.CompilerParams(
            dimension_semantics=("arbitrary", "arbitrary"), vmem_limit_bytes=VMEM_LIMIT),
        name="mlp",
    )(x2d, g, w1, w2)


def _lane_inv_freq(rot_dim):
    inv = ROPE_THETA ** (-jnp.arange(0, rot_dim, 2, dtype=F32) / rot_dim)
    return jnp.tile(inv, LANES // inv.shape[0])[None, :]


def _dsa_layer(x2d, pos2d, batch, seq, g, w_in, q_g, k_g, w_out, *, kc=512):
    w_main = w_in[:, :MAIN_W].astype(BF16)
    w_kw = jnp.pad(w_in[:, MAIN_W:], ((0, 0), (0, LANES - (IDX_DIM + IDX_HEADS)))).astype(BF16)
    main, kw = _attn_in(x2d, pos2d, g[None, :], w_main, w_kw, q_g[None, :], k_g[None, :],
                        _lane_inv_freq(ROPE_DIM), _lane_inv_freq(IDX_ROPE_DIM))
    main3 = main.reshape(batch, seq, MAIN_W)
    kw3 = kw.reshape(batch, seq, LANES)
    qt = jnp.swapaxes(main3[:, :, :Q_W], 1, 2)
    vt = main3[:, :, Q_W + KV_W:Q_W + 2 * KV_W].reshape(batch, seq // kc, kc, KV_W)
    vt = jnp.swapaxes(vt, 2, 3)
    qit = jnp.swapaxes(main3[:, :, Q_W + 2 * KV_W:], 1, 2)
    ki = kw3[:, :, :IDX_DIM].astype(BF16)
    wit = jnp.swapaxes(kw3[:, :, IDX_DIM:IDX_DIM + IDX_HEADS], 1, 2)
    o = _dsa(qit, wit, ki, qt, main3, vt, kc=kc)
    return _out_proj(o.reshape(batch * seq, Q_W), w_out.astype(BF16), x2d)


def kernel(x, positions, attn_norm_g, attn_w_in, attn_q_norm_g, attn_k_norm_g, attn_w_out,
           conv_norm_g, conv_w_in, conv_w, conv_w_out, mlp_norm_g, mlp_w1, mlp_w2):
    batch, seq, d = x.shape
    depth = mlp_w1.shape[0]
    x2d = x.reshape(batch * seq, d)
    pos2d = positions.reshape(batch * seq, 1)
    for i in range(depth):
        j = i // 2
        if i % 2 == 0:
            x2d = _dsa_layer(x2d, pos2d, batch, seq, attn_norm_g[j], attn_w_in[j],
                             attn_q_norm_g[j], attn_k_norm_g[j], attn_w_out[j])
        else:
            w_in = conv_w_in[j].astype(BF16)
            x2d = _conv_mixer(x2d, conv_norm_g[j][None, :], w_in[:, :d], w_in[:, d:2 * d],
                              w_in[:, 2 * d:], conv_w[j], conv_w_out[j].astype(BF16), seq=seq)
        x2d = _mlp(x2d, mlp_norm_g[i][None, :], mlp_w1[i].astype(BF16), mlp_w2[i].astype(BF16))
    return x2d.reshape(batch, seq, d)
```

```python
import functools
import math

import jax
import jax.numpy as jnp
from jax import lax
from jax.experimental import pallas as pl
from jax.experimental.pallas import tpu as pltpu

D_MODEL = 2048
N_HEADS = 16
N_KV_HEADS = 4
HEAD_DIM = 128
GROUP = N_HEADS // N_KV_HEADS
ROPE_DIM = HEAD_DIM // 4
ROPE_THETA = 500000.0
IDX_HEADS = 16
IDX_DIM = 64
IDX_ROPE_DIM = IDX_DIM // 4
TOPK_MAX = 256
Q_W = N_HEADS * HEAD_DIM
KV_W = N_KV_HEADS * HEAD_DIM
IDX_Q_W = IDX_HEADS * IDX_DIM
MAIN_W = Q_W + 2 * KV_W + IDX_Q_W
D_FF = 4 * D_MODEL
EPS = 1e-6
LANES = 128
SUBLANES = 8
VMEM_LIMIT = 56 * 1024 * 1024

INT_MIN = -(2 ** 31)
MASKED = -1e30
Q_SCALE_LOG2E = HEAD_DIM ** -0.5 * math.log2(math.e)

F32 = jnp.float32
BF16 = jnp.bfloat16


def _rmsnorm_rows(x, g):
    ms = jnp.mean(x * x, axis=-1, keepdims=True)
    return (x * lax.rsqrt(ms + EPS)) * g


def _rope_tables(pos, inv, period, half):
    ang = pos.astype(F32) * inv
    cos, sin = jnp.cos(ang), jnp.sin(ang)
    d = lax.broadcasted_iota(jnp.int32, ang.shape, 1) & (period - 1)
    c = jnp.where(d < 2 * half, cos, 1.0)
    s_lo = jnp.where(d < half, -sin, 0.0)
    s_hi = jnp.where((d >= half) & (d < 2 * half), sin, 0.0)
    return c, s_lo, s_hi


def _rope(x, c, s_lo, s_hi, half):
    return x * c + pltpu.roll(x, LANES - half, 1) * s_lo + pltpu.roll(x, half, 1) * s_hi


def _attn_in_kernel(x_ref, pos_ref, g_ref, w_ref, wkw_ref, qg_ref, kg_ref, invq_ref, invi_ref,
                    main_ref, kw_ref, h_ref, tq_ref, ti_ref, *, tn):
    j = pl.program_id(1)
    n_q = Q_W // tn
    n_k = KV_W // tn

    @pl.when(j == 0)
    def _():
        h = _rmsnorm_rows(x_ref[...], g_ref[...]).astype(BF16)
        h_ref[...] = h
        pos = pos_ref[...]
        cq, sq_lo, sq_hi = _rope_tables(pos, invq_ref[...], HEAD_DIM, ROPE_DIM // 2)
        tq_ref[0], tq_ref[1], tq_ref[2] = cq, sq_lo, sq_hi
        ci, si_lo, si_hi = _rope_tables(pos, invi_ref[...], IDX_DIM, IDX_ROPE_DIM // 2)
        ti_ref[0], ti_ref[1], ti_ref[2] = ci, si_lo, si_hi
        kw = jnp.dot(h, wkw_ref[...], preferred_element_type=F32)
        lane = lax.broadcasted_iota(jnp.int32, kw.shape, 1)
        roped = _rope(kw, ci, si_lo, si_hi, IDX_ROPE_DIM // 2)
        scale = IDX_HEADS ** -0.5 * IDX_DIM ** -0.5
        kw_ref[...] = jnp.where(lane < IDX_DIM, roped, kw * scale)

    acc = jnp.dot(h_ref[...], w_ref[...], preferred_element_type=F32)

    def heads_norm_rope(gain_ref, post_scale):
        for hh in range(tn // HEAD_DIM):
            sl = slice(hh * HEAD_DIM, (hh + 1) * HEAD_DIM)
            y = _rmsnorm_rows(acc[:, sl], gain_ref[...])
            y = _rope(y, tq_ref[0], tq_ref[1], tq_ref[2], ROPE_DIM // 2)
            if post_scale is not None:
                y = y * post_scale
            main_ref[:, sl] = y.astype(BF16)

    @pl.when(j < n_q)
    def _():
        heads_norm_rope(qg_ref, Q_SCALE_LOG2E)

    @pl.when((j >= n_q) & (j < n_q + n_k))
    def _():
        heads_norm_rope(kg_ref, None)

    @pl.when((j >= n_q + n_k) & (j < n_q + 2 * n_k))
    def _():
        main_ref[...] = acc.astype(BF16)

    @pl.when(j >= n_q + 2 * n_k)
    def _():
        for hh in range(tn // LANES):
            sl = slice(hh * LANES, (hh + 1) * LANES)
            y = _rope(acc[:, sl], ti_ref[0], ti_ref[1], ti_ref[2], IDX_ROPE_DIM // 2)
            main_ref[:, sl] = y.astype(BF16)


def _attn_in(x2d, pos2d, g, w_main, w_kw, q_g, k_g, inv_q, inv_i, *, tm=1024, tn=512):
    t = x2d.shape[0]
    grid = (t // tm, MAIN_W // tn)
    return pl.pallas_call(
        functools.partial(_attn_in_kernel, tn=tn),
        grid=grid,
        in_specs=[
            pl.BlockSpec((tm, D_MODEL), lambda i, j: (i, 0)),
            pl.BlockSpec((tm, 1), lambda i, j: (i, 0)),
            pl.BlockSpec((1, D_MODEL), lambda i, j: (0, 0)),
            pl.BlockSpec((D_MODEL, tn), lambda i, j: (0, j)),
            pl.BlockSpec((D_MODEL, LANES), lambda i, j: (0, 0)),
            pl.BlockSpec((1, HEAD_DIM), lambda i, j: (0, 0)),
            pl.BlockSpec((1, HEAD_DIM), lambda i, j: (0, 0)),
            pl.BlockSpec((1, LANES), lambda i, j: (0, 0)),
            pl.BlockSpec((1, LANES), lambda i, j: (0, 0)),
        ],
        out_specs=[
            pl.BlockSpec((tm, tn), lambda i, j: (i, j)),
            pl.BlockSpec((tm, LANES), lambda i, j: (i, 0)),
        ],
        out_shape=[
            jax.ShapeDtypeStruct((t, MAIN_W), BF16),
            jax.ShapeDtypeStruct((t, LANES), F32),
        ],
        scratch_shapes=[
            pltpu.VMEM((tm, D_MODEL), BF16),
            pltpu.VMEM((3, tm, LANES), F32),
            pltpu.VMEM((3, tm, LANES), F32),
        ],
        compiler_params=pltpu.CompilerParams(
            dimension_semantics=("arbitrary", "arbitrary"), vmem_limit_bytes=VMEM_LIMIT),
        name="attn_in",
    )(x2d, pos2d, g, w_main, w_kw, q_g, k_g, inv_q, inv_i)


def _dsa_kernel(qit_ref, wit_ref, ki_ref, qg_ref, k_ref, vt_ref, o_ref, key_ref, bias_ref,
                m_ref, acc_ref, *, tq, kc, top_k, idx_bits):
    i = pl.program_id(1)
    t0 = i * tq
    n_kc = (t0 + tq + kc - 1) // kc
    row = lax.broadcasted_iota(jnp.int32, (kc, tq), 0)
    t_idx = t0 + lax.broadcasted_iota(jnp.int32, (kc, tq), 1)

    def score_chunk(c, carry):
        k_rows = ki_ref[0, pl.ds(pl.multiple_of(c * kc, kc), kc), :]
        acc = jnp.zeros((kc, tq), F32)
        for h in range(IDX_HEADS):
            rel = jnp.dot(k_rows, qit_ref[0, h * IDX_DIM:(h + 1) * IDX_DIM, :],
                          preferred_element_type=F32)
            acc = acc + wit_ref[0, h:h + 1, :] * jnp.maximum(rel, 0.0)
        bits = pltpu.bitcast(acc, jnp.int32)
        key = bits ^ ((bits >> 31) & 0x7FFFFFFF)
        key_ref[c] = jnp.where(c * kc + row <= t_idx, key, INT_MIN)
        return carry

    lax.fori_loop(0, n_kc, score_chunk, 0)

    def count(pred_fn):
        def body(c, cnt):
            m = pred_fn(c, key_ref[c]).astype(jnp.int32)
            return cnt + m.reshape(kc // SUBLANES, SUBLANES, tq).sum(axis=0)
        part = lax.fori_loop(0, n_kc, body, jnp.zeros((SUBLANES, tq), jnp.int32))
        return part.sum(axis=0, keepdims=True)

    lo = jnp.where(count(lambda c, kk: kk >= 0) >= top_k, 0, INT_MIN).astype(jnp.int32)

    def bit_body(b, lo):
        cand = lo + jnp.left_shift(jnp.int32(1), 30 - b)
        return jnp.where(count(lambda c, kk: kk >= cand) >= top_k, cand, lo)

    thr = lax.fori_loop(0, 31, bit_body, lo)
    thr_sel = jnp.maximum(thr, INT_MIN + 1)

    def bias_chunk(c, cnt):
        sel = key_ref[c] >= thr_sel
        bias_ref[c] = jnp.where(sel, 0.0, MASKED)
        return cnt + sel.astype(jnp.int32).reshape(kc // SUBLANES, SUBLANES, tq).sum(axis=0)

    n_sel = lax.fori_loop(0, n_kc, bias_chunk, jnp.zeros((SUBLANES, tq), jnp.int32))
    n_sel = n_sel.sum(axis=0, keepdims=True)

    @pl.when(jnp.max(n_sel) > top_k)
    def _():
        need = top_k - count(lambda c, kk: kk > thr)

        def cut_body(b, p):
            cand = p + jnp.left_shift(jnp.int32(1), idx_bits - 1 - b)
            below = count(lambda c, kk: (kk == thr) & (c * kc + row < cand))
            return jnp.where(below < need, cand, p)

        cut = lax.fori_loop(0, idx_bits, cut_body, jnp.zeros((1, tq), jnp.int32))

        def rebias_chunk(c, carry):
            kk = key_ref[c]
            sel = (kk > thr) | ((kk == thr) & (c * kc + row <= cut))
            sel = sel & (kk != INT_MIN)
            bias_ref[c] = jnp.where(sel, 0.0, MASKED)
            return carry

        lax.fori_loop(0, n_kc, rebias_chunk, 0)

    m_ref[...] = jnp.full(m_ref.shape, MASKED, F32)
    acc_ref[...] = jnp.zeros(acc_ref.shape, F32)

    def attn_chunk(c, carry):
        bias = bias_ref[c]
        rows = pl.ds(pl.multiple_of(c * kc, kc), kc)
        for n in range(N_KV_HEADS):
            k_c = k_ref[0, rows, n * HEAD_DIM:(n + 1) * HEAD_DIM]
            s_all = jnp.dot(k_c, qg_ref[0, 0, n], preferred_element_type=F32)
            ps, alphas = [], []
            for g in range(GROUP):
                h = n * GROUP + g
                s = s_all[:, g * tq:(g + 1) * tq] + bias
                m_old = m_ref[h]
                m_new = jnp.maximum(m_old, jnp.max(s, axis=0, keepdims=True))
                ps.append(jnp.exp2((s - m_new).astype(BF16)))
                alphas.append(jnp.exp2(m_old - m_new))
                m_ref[h] = m_new
            p = jnp.concatenate(ps, axis=1)
            alpha = jnp.concatenate(alphas, axis=1)
            acc_ref[n] = alpha * acc_ref[n] + jnp.dot(
                vt_ref[0, c, n], p, preferred_element_type=F32)
        return carry

    lax.fori_loop(0, n_kc, attn_chunk, 0)
    for h in range(N_HEADS):
        n, g = divmod(h, GROUP)
        cols = slice(g * tq, (g + 1) * tq)
        o_t = acc_ref[n, :HEAD_DIM, cols] / acc_ref[n, HEAD_DIM:HEAD_DIM + 1, cols]
        o_ref[0, :, h * HEAD_DIM:(h + 1) * HEAD_DIM] = o_t.T.astype(BF16)


def _dsa(qit, wit, ki, qg, main3, vt, *, tq, kc):
    b, s, _ = main3.shape
    top_k = min(TOPK_MAX, s // 4)
    n_chunks = s // kc
    v_rows = vt.shape[3]
    k_col_block = Q_W // KV_W
    return pl.pallas_call(
        functools.partial(_dsa_kernel, tq=tq, kc=kc, top_k=top_k, idx_bits=(s - 1).bit_length()),
        grid=(b, s // tq),
        in_specs=[
            pl.BlockSpec((1, IDX_Q_W, tq), lambda bb, i: (bb, 0, i)),
            pl.BlockSpec((1, IDX_HEADS, tq), lambda bb, i: (bb, 0, i)),
            pl.BlockSpec((1, s, IDX_DIM), lambda bb, i: (bb, 0, 0)),
            pl.BlockSpec((1, 1, N_KV_HEADS, HEAD_DIM, GROUP * tq), lambda bb, i: (bb, i, 0, 0, 0)),
            pl.BlockSpec((1, s, KV_W), lambda bb, i: (bb, 0, k_col_block)),
            pl.BlockSpec((1, n_chunks, N_KV_HEADS, v_rows, kc), lambda bb, i: (bb, 0, 0, 0, 0)),
        ],
        out_specs=pl.BlockSpec((1, tq, Q_W), lambda bb, i: (bb, i, 0)),
        out_shape=jax.ShapeDtypeStruct((b, s, Q_W), BF16),
        scratch_shapes=[
            pltpu.VMEM((n_chunks, kc, tq), jnp.int32),
            pltpu.VMEM((n_chunks, kc, tq), F32),
            pltpu.VMEM((N_HEADS, 1, tq), F32),
            pltpu.VMEM((N_KV_HEADS, v_rows, GROUP * tq), F32),
        ],
        compiler_params=pltpu.CompilerParams(
            dimension_semantics=("arbitrary", "arbitrary"), vmem_limit_bytes=VMEM_LIMIT),
        name="dsa",
    )(qit, wit, ki, qg, main3, vt)


def _out_proj_kernel(a_ref, w_ref, x_ref, o_ref):
    o_ref[...] = x_ref[...] + jnp.dot(a_ref[...], w_ref[...], preferred_element_type=F32)


def _out_proj(a2d, w, x2d, *, tm=1024, tn=512):
    t, k = a2d.shape
    n = w.shape[1]
    return pl.pallas_call(
        _out_proj_kernel,
        grid=(t // tm, n // tn),
        in_specs=[
            pl.BlockSpec((tm, k), lambda i, j: (i, 0)),
            pl.BlockSpec((k, tn), lambda i, j: (0, j)),
            pl.BlockSpec((tm, tn), lambda i, j: (i, j)),
        ],
        out_specs=pl.BlockSpec((tm, tn), lambda i, j: (i, j)),
        out_shape=jax.ShapeDtypeStruct((t, n), F32),
        compiler_params=pltpu.CompilerParams(
            dimension_semantics=("arbitrary", "arbitrary"), vmem_limit_bytes=VMEM_LIMIT),
        name="out_proj",
    )(a2d, w, x2d)


def _conv_kernel(x_ref, g_ref, wb_ref, wc_ref, wu_ref, cw_ref, wo_ref, o_ref, h_ref, carry_ref,
                 *, tm, tiles_per_seq):
    i = pl.program_id(0)
    j = pl.program_id(1)

    @pl.when(j == 0)
    def _():
        x = x_ref[...]
        h_ref[...] = _rmsnorm_rows(x, g_ref[...]).astype(BF16)
        o_ref[...] = x

    h = h_ref[...]
    b_gate = jnp.dot(h, wb_ref[...], preferred_element_type=F32)
    c_gate = jnp.dot(h, wc_ref[...], preferred_element_type=F32)
    u = jnp.dot(h, wu_ref[...], preferred_element_type=F32)
    z = c_gate * u
    prev = jnp.where(i % tiles_per_seq == 0, 0.0, carry_ref[j])
    carry_ref[j] = z[tm - SUBLANES:, :]
    r = lax.broadcasted_iota(jnp.int32, z.shape, 0)
    z1 = jnp.where(r == 0, prev[7:8, :], pltpu.roll(z, 1, 0))
    z2 = jnp.where(r == 0, prev[6:7, :], jnp.where(r == 1, prev[7:8, :], pltpu.roll(z, 2, 0)))
    y = z2 * cw_ref[0:1, :] + z1 * cw_ref[1:2, :] + z * cw_ref[2:3, :]
    gated = (b_gate * y).astype(BF16)
    o_ref[...] += jnp.dot(gated, wo_ref[...], preferred_element_type=F32)


def _conv_mixer(x2d, g, wb, wc, wu, cw, wo, *, seq, tm=512, tn=512):
    t = x2d.shape[0]
    n_chunks = D_MODEL // tn
    return pl.pallas_call(
        functools.partial(_conv_kernel, tm=tm, tiles_per_seq=seq // tm),
        grid=(t // tm, n_chunks),
        in_specs=[
            pl.BlockSpec((tm, D_MODEL), lambda i, j: (i, 0)),
            pl.BlockSpec((1, D_MODEL), lambda i, j: (0, 0)),
            pl.BlockSpec((D_MODEL, tn), lambda i, j: (0, j)),
            pl.BlockSpec((D_MODEL, tn), lambda i, j: (0, j)),
            pl.BlockSpec((D_MODEL, tn), lambda i, j: (0, j)),
            pl.BlockSpec((3, tn), lambda i, j: (0, j)),
            pl.BlockSpec((tn, D_MODEL), lambda i, j: (j, 0)),
        ],
        out_specs=pl.BlockSpec((tm, D_MODEL), lambda i, j: (i, 0)),
        out_shape=jax.ShapeDtypeStruct((t, D_MODEL), F32),
        scratch_shapes=[
            pltpu.VMEM((tm, D_MODEL), BF16),
            pltpu.VMEM((n_chunks, SUBLANES, tn), F32),
        ],
        compiler_params=pltpu.CompilerParams(
            dimension_semantics=("arbitrary", "arbitrary"), vmem_limit_bytes=VMEM_LIMIT),
        name="conv_mixer",
    )(x2d, g, wb, wc, wu, cw, wo)


def _mlp_kernel(x_ref, g_ref, w1_ref, w2_ref, o_ref, h_ref):
    @pl.when(pl.program_id(1) == 0)
    def _():
        x = x_ref[...]
        h_ref[...] = _rmsnorm_rows(x, g_ref[...]).astype(BF16)
        o_ref[...] = x

    a = jnp.maximum(jnp.dot(h_ref[...], w1_ref[...], preferred_element_type=F32), 0.0)
    o_ref[...] += jnp.dot((a * a).astype(BF16), w2_ref[...], preferred_element_type=F32)


def _mlp(x2d, g, w1, w2, *, tm=512, tf=512):
    t = x2d.shape[0]
    return pl.pallas_call(
        _mlp_kernel,
        grid=(t // tm, D_FF // tf),
        in_specs=[
            pl.BlockSpec((tm, D_MODEL), lambda i, j: (i, 0)),
            pl.BlockSpec((1, D_MODEL), lambda i, j: (0, 0)),
            pl.BlockSpec((D_MODEL, tf), lambda i, j: (0, j)),
            pl.BlockSpec((tf, D_MODEL), lambda i, j: (j, 0)),
        ],
        out_specs=pl.BlockSpec((tm, D_MODEL), lambda i, j: (i, 0)),
        out_shape=jax.ShapeDtypeStruct((t, D_MODEL), F32),
        scratch_shapes=[pltpu.VMEM((tm, D_MODEL), BF16)],
        compiler_params=pltpu.CompilerParams(
            dimension_semantics=("arbitrary", "arbitrary"), vmem_limit_bytes=VMEM_LIMIT),
        name="mlp",
    )(x2d, g, w1, w2)


def _lane_inv_freq(rot_dim):
    inv = ROPE_THETA ** (-jnp.arange(0, rot_dim, 2, dtype=F32) / rot_dim)
    return jnp.tile(inv, LANES // inv.shape[0])[None, :]


def _dsa_layer(x2d, pos2d, batch, seq, g, w_in, q_g, k_g, w_out, *, tq=256, kc=512):
    w_main = w_in[:, :MAIN_W].astype(BF16)
    w_kw = jnp.pad(w_in[:, MAIN_W:], ((0, 0), (0, LANES - (IDX_DIM + IDX_HEADS)))).astype(BF16)
    main, kw = _attn_in(x2d, pos2d, g[None, :], w_main, w_kw, q_g[None, :], k_g[None, :],
                        _lane_inv_freq(ROPE_DIM), _lane_inv_freq(IDX_ROPE_DIM))
    main3 = main.reshape(batch, seq, MAIN_W)
    kw3 = kw.reshape(batch, seq, LANES)
    qg = main3[:, :, :Q_W].reshape(batch, seq // tq, tq, N_KV_HEADS, GROUP, HEAD_DIM)
    qg = qg.transpose(0, 1, 3, 5, 4, 2).reshape(batch, seq // tq, N_KV_HEADS, HEAD_DIM, GROUP * tq)
    vt = main3[:, :, Q_W + KV_W:Q_W + 2 * KV_W].reshape(batch, seq // kc, kc, N_KV_HEADS, HEAD_DIM)
    vt = vt.transpose(0, 1, 3, 4, 2)
    ones = jnp.ones(vt.shape[:3] + (2 * SUBLANES, kc), BF16)
    vt = jnp.concatenate([vt, ones], axis=3)
    qit = jnp.swapaxes(main3[:, :, Q_W + 2 * KV_W:], 1, 2)
    ki = kw3[:, :, :IDX_DIM].astype(BF16)
    wit = jnp.swapaxes(kw3[:, :, IDX_DIM:IDX_DIM + IDX_HEADS], 1, 2)
    o = _dsa(qit, wit, ki, qg, main3, vt, tq=tq, kc=kc)
    return _out_proj(o.reshape(batch * seq, Q_W), w_out.astype(BF16), x2d)


def kernel(x, positions, attn_norm_g, attn_w_in, attn_q_norm_g, attn_k_norm_g, attn_w_out,
           conv_norm_g, conv_w_in, conv_w, conv_w_out, mlp_norm_g, mlp_w1, mlp_w2):
    batch, seq, d = x.shape
    depth = mlp_w1.shape[0]
    x2d = x.reshape(batch * seq, d)
    pos2d = positions.reshape(batch * seq, 1)
    for i in range(depth):
        j = i // 2
        if i % 2 == 0:
            x2d = _dsa_layer(x2d, pos2d, batch, seq, attn_norm_g[j], attn_w_in[j],
                             attn_q_norm_g[j], attn_k_norm_g[j], attn_w_out[j])
        else:
            w_in = conv_w_in[j].astype(BF16)
            x2d = _conv_mixer(x2d, conv_norm_g[j][None, :], w_in[:, :d], w_in[:, d:2 * d],
                              w_in[:, 2 * d:], conv_w[j], conv_w_out[j].astype(BF16), seq=seq)
        x2d = _mlp(x2d, mlp_norm_g[i][None, :], mlp_w1[i].astype(BF16), mlp_w2[i].astype(BF16))
    return x2d.reshape(batch, seq, d)
```

```python
import functools
import math

import jax
import jax.numpy as jnp
from jax import lax
from jax.experimental import pallas as pl
from jax.experimental.pallas import tpu as pltpu

D_MODEL = 2048
N_HEADS = 16
N_KV_HEADS = 4
HEAD_DIM = 128
GROUP = N_HEADS // N_KV_HEADS
ROPE_DIM = HEAD_DIM // 4
ROPE_THETA = 500000.0
IDX_HEADS = 16
IDX_DIM = 64
IDX_ROPE_DIM = IDX_DIM // 4
TOPK_MAX = 256
Q_W = N_HEADS * HEAD_DIM
KV_W = N_KV_HEADS * HEAD_DIM
IDX_Q_W = IDX_HEADS * IDX_DIM
MAIN_W = Q_W + 2 * KV_W + IDX_Q_W
D_FF = 4 * D_MODEL
EPS = 1e-6
LANES = 128
SUBLANES = 8
VMEM_LIMIT = 56 * 1024 * 1024

INT_MIN = -(2 ** 31)
MASKED = -1e30
MAX_BISECT = 40
Q_SCALE_LOG2E = HEAD_DIM ** -0.5 * math.log2(math.e)

F32 = jnp.float32
BF16 = jnp.bfloat16


def _rmsnorm_rows(x, g):
    ms = jnp.mean(x * x, axis=-1, keepdims=True)
    return (x * lax.rsqrt(ms + EPS)) * g


def _rope_tables(pos, inv, period, half):
    ang = pos.astype(F32) * inv
    cos, sin = jnp.cos(ang), jnp.sin(ang)
    d = lax.broadcasted_iota(jnp.int32, ang.shape, 1) & (period - 1)
    c = jnp.where(d < 2 * half, cos, 1.0)
    s_lo = jnp.where(d < half, -sin, 0.0)
    s_hi = jnp.where((d >= half) & (d < 2 * half), sin, 0.0)
    return c, s_lo, s_hi


def _rope(x, c, s_lo, s_hi, half):
    return x * c + pltpu.roll(x, LANES - half, 1) * s_lo + pltpu.roll(x, half, 1) * s_hi


V_ONES_ROWS = 2 * SUBLANES


def _attn_in_kernel(x_ref, pos_ref, g_ref, w_ref, wkw_ref, qgain_ref, kgain_ref, invq_ref, invi_ref,
                    qg_ref, k_ref, vt_ref, qit_ref, ki_ref, wit_ref, h_ref, tq_ref, ti_ref,
                    *, tn, tq, kc):
    j = pl.program_id(1)
    tm = x_ref.shape[0]
    n_q = Q_W // tn
    n_k = KV_W // tn

    @pl.when(j == 0)
    def _():
        h = _rmsnorm_rows(x_ref[...], g_ref[...]).astype(BF16)
        h_ref[...] = h
        pos = pos_ref[...]
        cq, sq_lo, sq_hi = _rope_tables(pos, invq_ref[...], HEAD_DIM, ROPE_DIM // 2)
        tq_ref[0], tq_ref[1], tq_ref[2] = cq, sq_lo, sq_hi
        ci, si_lo, si_hi = _rope_tables(pos, invi_ref[...], IDX_DIM, IDX_ROPE_DIM // 2)
        ti_ref[0], ti_ref[1], ti_ref[2] = ci, si_lo, si_hi
        kw = jnp.dot(h, wkw_ref[...], preferred_element_type=F32)
        roped = _rope(kw, ci, si_lo, si_hi, IDX_ROPE_DIM // 2)
        ki_ref[...] = roped[:, :IDX_DIM].astype(BF16)
        scale = IDX_HEADS ** -0.5 * IDX_DIM ** -0.5
        wit_ref[0] = (kw * scale).T[IDX_DIM:IDX_DIM + IDX_HEADS, :]

    acc = jnp.dot(h_ref[...], w_ref[...], preferred_element_type=F32)

    def head_norm_rope(hh, gain_ref):
        y = _rmsnorm_rows(acc[:, hh * HEAD_DIM:(hh + 1) * HEAD_DIM], gain_ref[...])
        return _rope(y, tq_ref[0], tq_ref[1], tq_ref[2], ROPE_DIM // 2)

    @pl.when(j < n_q)
    def _():
        for g in range(GROUP):
            y = head_norm_rope(g, qgain_ref) * Q_SCALE_LOG2E
            for r in range(tm // tq):
                qg_ref[r, 0, :, g * tq:(g + 1) * tq] = y[r * tq:(r + 1) * tq, :].T.astype(BF16)

    @pl.when((j >= n_q) & (j < n_q + n_k))
    def _():
        for hh in range(N_KV_HEADS):
            k_ref[:, hh * HEAD_DIM:(hh + 1) * HEAD_DIM] = head_norm_rope(hh, kgain_ref).astype(BF16)

    @pl.when((j >= n_q + n_k) & (j < n_q + 2 * n_k))
    def _():
        for n in range(N_KV_HEADS):
            for cc in range(tm // kc):
                v = acc[cc * kc:(cc + 1) * kc, n * HEAD_DIM:(n + 1) * HEAD_DIM]
                vt_ref[cc, n, :HEAD_DIM, :] = v.T.astype(BF16)
                vt_ref[cc, n, HEAD_DIM:, :] = jnp.ones((V_ONES_ROWS, kc), BF16)

    @pl.when(j >= n_q + 2 * n_k)
    def _():
        for hh in range(tn // LANES):
            sl = slice(hh * LANES, (hh + 1) * LANES)
            y = _rope(acc[:, sl], ti_ref[0], ti_ref[1], ti_ref[2], IDX_ROPE_DIM // 2)
            qit_ref[0, sl, :] = y.T.astype(BF16)


def _attn_in(x2d, pos2d, g, w_main, w_kw, q_g, k_g, inv_q, inv_i, *, batch, seq, tq, kc,
             tm=1024, tn=512):
    t = x2d.shape[0]
    assert tn == KV_W == GROUP * HEAD_DIM and tm % tq == 0 and tm % kc == 0 and seq % tm == 0
    tiles_per_seq = seq // tm
    n_q = Q_W // tn
    first_qi = MAIN_W // tn - IDX_Q_W // tn
    grid = (t // tm, MAIN_W // tn)
    return pl.pallas_call(
        functools.partial(_attn_in_kernel, tn=tn, tq=tq, kc=kc),
        grid=grid,
        in_specs=[
            pl.BlockSpec((tm, D_MODEL), lambda i, j: (i, 0)),
            pl.BlockSpec((tm, 1), lambda i, j: (i, 0)),
            pl.BlockSpec((1, D_MODEL), lambda i, j: (0, 0)),
            pl.BlockSpec((D_MODEL, tn), lambda i, j: (0, j)),
            pl.BlockSpec((D_MODEL, LANES), lambda i, j: (0, 0)),
            pl.BlockSpec((1, HEAD_DIM), lambda i, j: (0, 0)),
            pl.BlockSpec((1, HEAD_DIM), lambda i, j: (0, 0)),
            pl.BlockSpec((1, LANES), lambda i, j: (0, 0)),
            pl.BlockSpec((1, LANES), lambda i, j: (0, 0)),
        ],
        out_specs=[
            pl.BlockSpec((tm // tq, 1, HEAD_DIM, GROUP * tq),
                         lambda i, j: (i, jnp.minimum(j, n_q - 1), 0, 0)),
            pl.BlockSpec((tm, KV_W), lambda i, j: (i, 0)),
            pl.BlockSpec((tm // kc, N_KV_HEADS, HEAD_DIM + V_ONES_ROWS, kc), lambda i, j: (i, 0, 0, 0)),
            pl.BlockSpec((1, tn, tm),
                         lambda i, j: (i // tiles_per_seq, jnp.maximum(j - first_qi, 0), i % tiles_per_seq)),
            pl.BlockSpec((tm, IDX_DIM), lambda i, j: (i, 0)),
            pl.BlockSpec((1, IDX_HEADS, tm), lambda i, j: (i // tiles_per_seq, 0, i % tiles_per_seq)),
        ],
        out_shape=[
            jax.ShapeDtypeStruct((t // tq, N_KV_HEADS, HEAD_DIM, GROUP * tq), BF16),
            jax.ShapeDtypeStruct((t, KV_W), BF16),
            jax.ShapeDtypeStruct((t // kc, N_KV_HEADS, HEAD_DIM + V_ONES_ROWS, kc), BF16),
            jax.ShapeDtypeStruct((batch, IDX_Q_W, seq), BF16),
            jax.ShapeDtypeStruct((t, IDX_DIM), BF16),
            jax.ShapeDtypeStruct((batch, IDX_HEADS, seq), F32),
        ],
        scratch_shapes=[
            pltpu.VMEM((tm, D_MODEL), BF16),
            pltpu.VMEM((3, tm, LANES), F32),
            pltpu.VMEM((3, tm, LANES), F32),
        ],
        compiler_params=pltpu.CompilerParams(
            dimension_semantics=("arbitrary", "arbitrary"), vmem_limit_bytes=VMEM_LIMIT),
        name="attn_in",
    )(x2d, pos2d, g, w_main, w_kw, q_g, k_g, inv_q, inv_i)


def _fold8(x, op):
    rows, cols = x.shape
    x = x.reshape(rows // SUBLANES, SUBLANES, cols)
    if op is jnp.add:
        return x.sum(axis=0)
    return x.min(axis=0) if op is jnp.minimum else x.max(axis=0)


def _dsa_kernel(qit_ref, wit_ref, ki_ref, qg_ref, k_ref, vt_ref, o_ref, sc_ref, bias_ref,
                m_ref, acc_ref, *, tq, kc, top_k, idx_bits):
    i = pl.program_id(1)
    t0 = i * tq
    n_kc = (t0 + tq + kc - 1) // kc
    row = lax.broadcasted_iota(jnp.int32, (kc, tq), 0)
    t_idx = t0 + lax.broadcasted_iota(jnp.int32, (kc, tq), 1)

    def score_chunk(c, carry):
        lo8, hi8 = carry
        k_rows = ki_ref[0, pl.ds(pl.multiple_of(c * kc, kc), kc), :]
        acc = jnp.zeros((kc, tq), F32)
        for h in range(IDX_HEADS):
            rel = jnp.dot(k_rows, qit_ref[0, h * IDX_DIM:(h + 1) * IDX_DIM, :],
                          preferred_element_type=F32)
            acc = acc + wit_ref[0, h:h + 1, :] * jnp.maximum(rel, 0.0)
        causal = c * kc + row <= t_idx
        sc_ref[c] = jnp.where(causal, acc, -jnp.inf)
        lo8 = jnp.minimum(lo8, _fold8(jnp.where(causal, acc, jnp.inf), jnp.minimum))
        hi8 = jnp.maximum(hi8, _fold8(jnp.where(causal, acc, -jnp.inf), jnp.maximum))
        return lo8, hi8

    lo8, hi8 = lax.fori_loop(0, n_kc, score_chunk, (jnp.full((SUBLANES, tq), jnp.inf, F32),
                                                     jnp.full((SUBLANES, tq), -jnp.inf, F32)))
    row_min = jnp.min(lo8, axis=0, keepdims=True)
    row_max = jnp.max(hi8, axis=0, keepdims=True)

    def count(pred_fn):
        def body(c, cnt):
            return cnt + _fold8(pred_fn(c, sc_ref[c]).astype(jnp.int32), jnp.add)
        part = lax.fori_loop(0, n_kc, body, jnp.zeros((SUBLANES, tq), jnp.int32))
        return part.sum(axis=0, keepdims=True)

    n_causal = t0 + 1 + lax.broadcasted_iota(jnp.int32, (1, tq), 1)
    short = n_causal <= top_k
    max_ties = count(lambda c, sc: sc >= row_max) >= top_k
    thr0 = jnp.where(short, row_min, row_max)
    done0 = (short | max_ties).astype(jnp.int32)

    def bisect_cond(st):
        it, _, _, _, done = st
        return (it < MAX_BISECT) & (jnp.min(done) == 0)

    def bisect_body(st):
        it, lo, hi, thr, done = st
        mid = 0.5 * lo + 0.5 * hi
        inside = (mid > lo) & (mid < hi)
        cnt = count(lambda c, sc: sc >= mid)
        active = done == 0
        hit = active & inside & (cnt == top_k)
        converged = active & ~inside
        thr = jnp.where(hit, mid, jnp.where(converged, lo, thr))
        done = jnp.where(hit | converged, 1, done)
        lo = jnp.where(active & inside & (cnt > top_k), mid, lo)
        hi = jnp.where(active & inside & (cnt < top_k), mid, hi)
        return it + 1, lo, hi, thr, done

    _, _, _, thr, done = lax.while_loop(bisect_cond, bisect_body,
                                        (jnp.int32(0), row_min, row_max, thr0, done0))

    def to_key(x):
        bits = pltpu.bitcast(x, jnp.int32)
        return bits ^ ((bits >> 31) & 0x7FFFFFFF)

    def exact_kth(_):
        lo = jnp.where(count(lambda c, sc: to_key(sc) >= 0) >= top_k, 0, INT_MIN).astype(jnp.int32)

        def bit_body(b, lo):
            cand = lo + jnp.left_shift(jnp.int32(1), 30 - b)
            return jnp.where(count(lambda c, sc: to_key(sc) >= cand) >= top_k, cand, lo)

        kth = lax.fori_loop(0, 31, bit_body, lo)
        kth = pltpu.bitcast(kth ^ ((kth >> 31) & 0x7FFFFFFF), F32)
        return jnp.where(done == 0, kth, thr)

    thr = lax.cond(jnp.min(done) == 0, exact_kth, lambda _: thr, 0)

    def bias_chunk(c, cnt):
        sel = sc_ref[c] >= thr
        bias_ref[c] = jnp.where(sel, 0.0, MASKED)
        return cnt + _fold8(sel.astype(jnp.int32), jnp.add)

    n_sel = lax.fori_loop(0, n_kc, bias_chunk, jnp.zeros((SUBLANES, tq), jnp.int32))
    n_sel = n_sel.sum(axis=0, keepdims=True)

    @pl.when(jnp.max(n_sel) > top_k)
    def _():
        need = top_k - count(lambda c, sc: sc > thr)

        def cut_body(b, p):
            cand = p + jnp.left_shift(jnp.int32(1), idx_bits - 1 - b)
            below = count(lambda c, sc: (sc == thr) & (c * kc + row < cand))
            return jnp.where(below < need, cand, p)

        cut = lax.fori_loop(0, idx_bits, cut_body, jnp.zeros((1, tq), jnp.int32))

        def rebias_chunk(c, carry):
            sc = sc_ref[c]
            sel = (sc > thr) | ((sc == thr) & (c * kc + row <= cut))
            bias_ref[c] = jnp.where(sel, 0.0, MASKED)
            return carry

        lax.fori_loop(0, n_kc, rebias_chunk, 0)

    m_ref[...] = jnp.full(m_ref.shape, MASKED, F32)
    acc_ref[...] = jnp.zeros(acc_ref.shape, F32)

    def attn_chunk(c, carry):
        bias = bias_ref[c]
        rows = pl.ds(pl.multiple_of(c * kc, kc), kc)
        for n in range(N_KV_HEADS):
            k_c = k_ref[0, rows, n * HEAD_DIM:(n + 1) * HEAD_DIM]
            s_all = jnp.dot(k_c, qg_ref[0, n], preferred_element_type=F32)
            ps, alphas = [], []
            for g in range(GROUP):
                h = n * GROUP + g
                s = s_all[:, g * tq:(g + 1) * tq] + bias
                m_old = m_ref[h]
                m_new = jnp.maximum(m_old, jnp.max(s, axis=0, keepdims=True))
                ps.append(jnp.exp2((s - m_new).astype(BF16)))
                alphas.append(jnp.exp2(m_old - m_new))
                m_ref[h] = m_new
            p = jnp.concatenate(ps, axis=1)
            alpha = jnp.concatenate(alphas, axis=1)
            acc_ref[n] = alpha * acc_ref[n] + jnp.dot(
                vt_ref[0, c, n], p, preferred_element_type=F32)
        return carry

    lax.fori_loop(0, n_kc, attn_chunk, 0)
    for h in range(N_HEADS):
        n, g = divmod(h, GROUP)
        cols = slice(g * tq, (g + 1) * tq)
        o_t = acc_ref[n, :HEAD_DIM, cols] / acc_ref[n, HEAD_DIM:HEAD_DIM + 1, cols]
        o_ref[0, :, h * HEAD_DIM:(h + 1) * HEAD_DIM] = o_t.T.astype(BF16)


def _dsa(qit, wit, ki, qg, k, vt, *, tq, kc):
    b, s, _ = k.shape
    top_k = min(TOPK_MAX, s // 4)
    n_chunks = s // kc
    n_qb = s // tq
    v_rows = vt.shape[3]
    return pl.pallas_call(
        functools.partial(_dsa_kernel, tq=tq, kc=kc, top_k=top_k, idx_bits=(s - 1).bit_length()),
        grid=(b, s // tq),
        in_specs=[
            pl.BlockSpec((1, IDX_Q_W, tq), lambda bb, i: (bb, 0, i)),
            pl.BlockSpec((1, IDX_HEADS, tq), lambda bb, i: (bb, 0, i)),
            pl.BlockSpec((1, s, IDX_DIM), lambda bb, i: (bb, 0, 0)),
            pl.BlockSpec((1, N_KV_HEADS, HEAD_DIM, GROUP * tq), lambda bb, i: (bb * n_qb + i, 0, 0, 0)),
            pl.BlockSpec((1, s, KV_W), lambda bb, i: (bb, 0, 0)),
            pl.BlockSpec((1, n_chunks, N_KV_HEADS, v_rows, kc), lambda bb, i: (bb, 0, 0, 0, 0)),
        ],
        out_specs=pl.BlockSpec((1, tq, Q_W), lambda bb, i: (bb, i, 0)),
        out_shape=jax.ShapeDtypeStruct((b, s, Q_W), BF16),
        scratch_shapes=[
            pltpu.VMEM((n_chunks, kc, tq), F32),
            pltpu.VMEM((n_chunks, kc, tq), F32),
            pltpu.VMEM((N_HEADS, 1, tq), F32),
            pltpu.VMEM((N_KV_HEADS, v_rows, GROUP * tq), F32),
        ],
        compiler_params=pltpu.CompilerParams(
            dimension_semantics=("arbitrary", "arbitrary"), vmem_limit_bytes=VMEM_LIMIT),
        name="dsa",
    )(qit, wit, ki, qg, k, vt)


def _out_proj_kernel(a_ref, w_ref, x_ref, o_ref):
    o_ref[...] = x_ref[...] + jnp.dot(a_ref[...], w_ref[...], preferred_element_type=F32)


def _out_proj(a2d, w, x2d, *, tm=1024, tn=512):
    t, k = a2d.shape
    n = w.shape[1]
    return pl.pallas_call(
        _out_proj_kernel,
        grid=(t // tm, n // tn),
        in_specs=[
            pl.BlockSpec((tm, k), lambda i, j: (i, 0)),
            pl.BlockSpec((k, tn), lambda i, j: (0, j)),
            pl.BlockSpec((tm, tn), lambda i, j: (i, j)),
        ],
        out_specs=pl.BlockSpec((tm, tn), lambda i, j: (i, j)),
        out_shape=jax.ShapeDtypeStruct((t, n), F32),
        compiler_params=pltpu.CompilerParams(
            dimension_semantics=("arbitrary", "arbitrary"), vmem_limit_bytes=VMEM_LIMIT),
        name="out_proj",
    )(a2d, w, x2d)


def _conv_kernel(x_ref, g_ref, wb_ref, wc_ref, wu_ref, cw_ref, wo_ref, o_ref, h_ref, carry_ref,
                 *, tm, tiles_per_seq):
    i = pl.program_id(0)
    j = pl.program_id(1)

    @pl.when(j == 0)
    def _():
        x = x_ref[...]
        h_ref[...] = _rmsnorm_rows(x, g_ref[...]).astype(BF16)
        o_ref[...] = x

    h = h_ref[...]
    b_gate = jnp.dot(h, wb_ref[...], preferred_element_type=F32)
    c_gate = jnp.dot(h, wc_ref[...], preferred_element_type=F32)
    u = jnp.dot(h, wu_ref[...], preferred_element_type=F32)
    z = c_gate * u
    prev = jnp.where(i % tiles_per_seq == 0, 0.0, carry_ref[j])
    carry_ref[j] = z[tm - SUBLANES:, :]
    r = lax.broadcasted_iota(jnp.int32, z.shape, 0)
    z1 = jnp.where(r == 0, prev[7:8, :], pltpu.roll(z, 1, 0))
    z2 = jnp.where(r == 0, prev[6:7, :], jnp.where(r == 1, prev[7:8, :], pltpu.roll(z, 2, 0)))
    y = z2 * cw_ref[0:1, :] + z1 * cw_ref[1:2, :] + z * cw_ref[2:3, :]
    gated = (b_gate * y).astype(BF16)
    o_ref[...] += jnp.dot(gated, wo_ref[...], preferred_element_type=F32)


def _conv_mixer(x2d, g, wb, wc, wu, cw, wo, *, seq, tm=512, tn=512):
    t = x2d.shape[0]
    n_chunks = D_MODEL // tn
    return pl.pallas_call(
        functools.partial(_conv_kernel, tm=tm, tiles_per_seq=seq // tm),
        grid=(t // tm, n_chunks),
        in_specs=[
            pl.BlockSpec((tm, D_MODEL), lambda i, j: (i, 0)),
            pl.BlockSpec((1, D_MODEL), lambda i, j: (0, 0)),
            pl.BlockSpec((D_MODEL, tn), lambda i, j: (0, j)),
            pl.BlockSpec((D_MODEL, tn), lambda i, j: (0, j)),
            pl.BlockSpec((D_MODEL, tn), lambda i, j: (0, j)),
            pl.BlockSpec((3, tn), lambda i, j: (0, j)),
            pl.BlockSpec((tn, D_MODEL), lambda i, j: (j, 0)),
        ],
        out_specs=pl.BlockSpec((tm, D_MODEL), lambda i, j: (i, 0)),
        out_shape=jax.ShapeDtypeStruct((t, D_MODEL), F32),
        scratch_shapes=[
            pltpu.VMEM((tm, D_MODEL), BF16),
            pltpu.VMEM((n_chunks, SUBLANES, tn), F32),
        ],
        compiler_params=pltpu.CompilerParams(
            dimension_semantics=("arbitrary", "arbitrary"), vmem_limit_bytes=VMEM_LIMIT),
        name="conv_mixer",
    )(x2d, g, wb, wc, wu, cw, wo)


def _mlp_kernel(x_ref, g_ref, w1_ref, w2_ref, o_ref, h_ref):
    @pl.when(pl.program_id(1) == 0)
    def _():
        x = x_ref[...]
        h_ref[...] = _rmsnorm_rows(x, g_ref[...]).astype(BF16)
        o_ref[...] = x

    a = jnp.maximum(jnp.dot(h_ref[...], w1_ref[...], preferred_element_type=F32), 0.0)
    o_ref[...] += jnp.dot((a * a).astype(BF16), w2_ref[...], preferred_element_type=F32)


def _mlp(x2d, g, w1, w2, *, tm=512, tf=1024):
    t = x2d.shape[0]
    return pl.pallas_call(
        _mlp_kernel,
        grid=(t // tm, D_FF // tf),
        in_specs=[
            pl.BlockSpec((tm, D_MODEL), lambda i, j: (i, 0)),
            pl.BlockSpec((1, D_MODEL), lambda i, j: (0, 0)),
            pl.BlockSpec((D_MODEL, tf), lambda i, j: (0, j)),
            pl.BlockSpec((tf, D_MODEL), lambda i, j: (j, 0)),
        ],
        out_specs=pl.BlockSpec((tm, D_MODEL), lambda i, j: (i, 0)),
        out_shape=jax.ShapeDtypeStruct((t, D_MODEL), F32),
        scratch_shapes=[pltpu.VMEM((tm, D_MODEL), BF16)],
        compiler_params=pltpu.CompilerParams(
            dimension_semantics=("arbitrary", "arbitrary"), vmem_limit_bytes=VMEM_LIMIT),
        name="mlp",
    )(x2d, g, w1, w2)


def _lane_inv_freq(rot_dim):
    inv = ROPE_THETA ** (-jnp.arange(0, rot_dim, 2, dtype=F32) / rot_dim)
    return jnp.tile(inv, LANES // inv.shape[0])[None, :]


def _dsa_layer(x2d, pos2d, batch, seq, g, w_in, q_g, k_g, w_out, *, tq=256, kc=512):
    w_main = w_in[:, :MAIN_W].astype(BF16)
    w_kw = jnp.pad(w_in[:, MAIN_W:], ((0, 0), (0, LANES - (IDX_DIM + IDX_HEADS)))).astype(BF16)
    qg, k, vt, qit, ki, wit = _attn_in(
        x2d, pos2d, g[None, :], w_main, w_kw, q_g[None, :], k_g[None, :],
        _lane_inv_freq(ROPE_DIM), _lane_inv_freq(IDX_ROPE_DIM), batch=batch, seq=seq, tq=tq, kc=kc)
    o = _dsa(qit, wit, ki.reshape(batch, seq, IDX_DIM), qg, k.reshape(batch, seq, KV_W),
             vt.reshape((batch, seq // kc) + vt.shape[1:]), tq=tq, kc=kc)
    return _out_proj(o.reshape(batch * seq, Q_W), w_out.astype(BF16), x2d)


def kernel(x, positions, attn_norm_g, attn_w_in, attn_q_norm_g, attn_k_norm_g, attn_w_out,
           conv_norm_g, conv_w_in, conv_w, conv_w_out, mlp_norm_g, mlp_w1, mlp_w2):
    batch, seq, d = x.shape
    depth = mlp_w1.shape[0]
    x2d = x.reshape(batch * seq, d)
    pos2d = positions.reshape(batch * seq, 1)
    for i in range(depth):
        j = i // 2
        if i % 2 == 0:
            x2d = _dsa_layer(x2d, pos2d, batch, seq, attn_norm_g[j], attn_w_in[j],
                             attn_q_norm_g[j], attn_k_norm_g[j], attn_w_out[j])
        else:
            w_in = conv_w_in[j].astype(BF16)
            x2d = _conv_mixer(x2d, conv_norm_g[j][None, :], w_in[:, :d], w_in[:, d:2 * d],
                              w_in[:, 2 * d:], conv_w[j], conv_w_out[j].astype(BF16), seq=seq)
        x2d = _mlp(x2d, mlp_norm_g[i][None, :], mlp_w1[i].astype(BF16), mlp_w2[i].astype(BF16))
    return x2d.reshape(batch, seq, d)
```

```python
import functools
import math

import jax
import jax.numpy as jnp
from jax import lax
from jax.experimental import pallas as pl
from jax.experimental.pallas import tpu as pltpu

D_MODEL = 2048
N_HEADS = 16
N_KV_HEADS = 4
HEAD_DIM = 128
GROUP = N_HEADS // N_KV_HEADS
ROPE_DIM = HEAD_DIM // 4
ROPE_THETA = 500000.0
IDX_HEADS = 16
IDX_DIM = 64
IDX_ROPE_DIM = IDX_DIM // 4
TOPK_MAX = 256
Q_W = N_HEADS * HEAD_DIM
KV_W = N_KV_HEADS * HEAD_DIM
IDX_Q_W = IDX_HEADS * IDX_DIM
MAIN_W = Q_W + 2 * KV_W + IDX_Q_W
D_FF = 4 * D_MODEL
EPS = 1e-6
LANES = 128
SUBLANES = 8
VMEM_LIMIT = 56 * 1024 * 1024

INT_MIN = -(2 ** 31)
MASKED = -1e30
MAX_BISECT = 40
Q_SCALE_LOG2E = HEAD_DIM ** -0.5 * math.log2(math.e)

F32 = jnp.float32
BF16 = jnp.bfloat16


def _rmsnorm_rows(x, g):
    ms = jnp.mean(x * x, axis=-1, keepdims=True)
    return (x * lax.rsqrt(ms + EPS)) * g


def _rope_tables(pos, inv, period, half):
    ang = pos.astype(F32) * inv
    cos, sin = jnp.cos(ang), jnp.sin(ang)
    d = lax.broadcasted_iota(jnp.int32, ang.shape, 1) & (period - 1)
    c = jnp.where(d < 2 * half, cos, 1.0)
    s_lo = jnp.where(d < half, -sin, 0.0)
    s_hi = jnp.where((d >= half) & (d < 2 * half), sin, 0.0)
    return c, s_lo, s_hi


def _rope(x, c, s_lo, s_hi, half):
    return x * c + pltpu.roll(x, LANES - half, 1) * s_lo + pltpu.roll(x, half, 1) * s_hi


V_ONES_ROWS = 2 * SUBLANES


def _attn_in_kernel(x_ref, pos_ref, g_ref, w_ref, wkw_ref, qgain_ref, kgain_ref, invq_ref, invi_ref,
                    qg_ref, k_ref, vt_ref, qit_ref, ki_ref, wit_ref, h_ref, tq_ref, ti_ref,
                    *, tn, tq, kc):
    j = pl.program_id(1)
    tm = x_ref.shape[0]
    n_q = Q_W // tn
    n_k = KV_W // tn

    @pl.when(j == 0)
    def _():
        h = _rmsnorm_rows(x_ref[...], g_ref[...]).astype(BF16)
        h_ref[...] = h
        pos = pos_ref[...]
        cq, sq_lo, sq_hi = _rope_tables(pos, invq_ref[...], HEAD_DIM, ROPE_DIM // 2)
        tq_ref[0], tq_ref[1], tq_ref[2] = cq, sq_lo, sq_hi
        ci, si_lo, si_hi = _rope_tables(pos, invi_ref[...], IDX_DIM, IDX_ROPE_DIM // 2)
        ti_ref[0], ti_ref[1], ti_ref[2] = ci, si_lo, si_hi
        kw = jnp.dot(h, wkw_ref[...], preferred_element_type=F32)
        roped = _rope(kw, ci, si_lo, si_hi, IDX_ROPE_DIM // 2)
        ki_ref[...] = roped[:, :IDX_DIM].astype(BF16)
        scale = IDX_HEADS ** -0.5 * IDX_DIM ** -0.5
        wit_ref[0] = (kw * scale).T[IDX_DIM:IDX_DIM + IDX_HEADS, :]

    acc = jnp.dot(h_ref[...], w_ref[...], preferred_element_type=F32)

    def head_norm_rope(hh, gain_ref):
        y = _rmsnorm_rows(acc[:, hh * HEAD_DIM:(hh + 1) * HEAD_DIM], gain_ref[...])
        return _rope(y, tq_ref[0], tq_ref[1], tq_ref[2], ROPE_DIM // 2)

    @pl.when(j < n_q)
    def _():
        for g in range(GROUP):
            y = head_norm_rope(g, qgain_ref) * Q_SCALE_LOG2E
            for r in range(tm // tq):
                qg_ref[r, 0, :, g * tq:(g + 1) * tq] = y[r * tq:(r + 1) * tq, :].T.astype(BF16)

    @pl.when((j >= n_q) & (j < n_q + n_k))
    def _():
        for hh in range(N_KV_HEADS):
            k_ref[:, hh * HEAD_DIM:(hh + 1) * HEAD_DIM] = head_norm_rope(hh, kgain_ref).astype(BF16)

    @pl.when((j >= n_q + n_k) & (j < n_q + 2 * n_k))
    def _():
        for n in range(N_KV_HEADS):
            for cc in range(tm // kc):
                v = acc[cc * kc:(cc + 1) * kc, n * HEAD_DIM:(n + 1) * HEAD_DIM]
                vt_ref[cc, n, :HEAD_DIM, :] = v.T.astype(BF16)
                vt_ref[cc, n, HEAD_DIM:, :] = jnp.ones((V_ONES_ROWS, kc), BF16)

    @pl.when(j >= n_q + 2 * n_k)
    def _():
        for hh in range(tn // LANES):
            sl = slice(hh * LANES, (hh + 1) * LANES)
            y = _rope(acc[:, sl], ti_ref[0], ti_ref[1], ti_ref[2], IDX_ROPE_DIM // 2)
            qit_ref[0, sl, :] = y.T.astype(BF16)


def _attn_in(x2d, pos2d, g, w_main, w_kw, q_g, k_g, inv_q, inv_i, *, batch, seq, tq, kc,
             tm=1024, tn=512):
    t = x2d.shape[0]
    assert tn == KV_W == GROUP * HEAD_DIM and tm % tq == 0 and tm % kc == 0 and seq % tm == 0
    tiles_per_seq = seq // tm
    n_q = Q_W // tn
    first_qi = MAIN_W // tn - IDX_Q_W // tn
    grid = (t // tm, MAIN_W // tn)
    return pl.pallas_call(
        functools.partial(_attn_in_kernel, tn=tn, tq=tq, kc=kc),
        grid=grid,
        in_specs=[
            pl.BlockSpec((tm, D_MODEL), lambda i, j: (i, 0)),
            pl.BlockSpec((tm, 1), lambda i, j: (i, 0)),
            pl.BlockSpec((1, D_MODEL), lambda i, j: (0, 0)),
            pl.BlockSpec((D_MODEL, tn), lambda i, j: (0, j)),
            pl.BlockSpec((D_MODEL, LANES), lambda i, j: (0, 0)),
            pl.BlockSpec((1, HEAD_DIM), lambda i, j: (0, 0)),
            pl.BlockSpec((1, HEAD_DIM), lambda i, j: (0, 0)),
            pl.BlockSpec((1, LANES), lambda i, j: (0, 0)),
            pl.BlockSpec((1, LANES), lambda i, j: (0, 0)),
        ],
        out_specs=[
            pl.BlockSpec((tm // tq, 1, HEAD_DIM, GROUP * tq),
                         lambda i, j: (i, jnp.minimum(j, n_q - 1), 0, 0)),
            pl.BlockSpec((tm, KV_W), lambda i, j: (i, 0)),
            pl.BlockSpec((tm // kc, N_KV_HEADS, HEAD_DIM + V_ONES_ROWS, kc), lambda i, j: (i, 0, 0, 0)),
            pl.BlockSpec((1, tn, tm),
                         lambda i, j: (i // tiles_per_seq, jnp.maximum(j - first_qi, 0), i % tiles_per_seq)),
            pl.BlockSpec((tm, IDX_DIM), lambda i, j: (i, 0)),
            pl.BlockSpec((1, IDX_HEADS, tm), lambda i, j: (i // tiles_per_seq, 0, i % tiles_per_seq)),
        ],
        out_shape=[
            jax.ShapeDtypeStruct((t // tq, N_KV_HEADS, HEAD_DIM, GROUP * tq), BF16),
            jax.ShapeDtypeStruct((t, KV_W), BF16),
            jax.ShapeDtypeStruct((t // kc, N_KV_HEADS, HEAD_DIM + V_ONES_ROWS, kc), BF16),
            jax.ShapeDtypeStruct((batch, IDX_Q_W, seq), BF16),
            jax.ShapeDtypeStruct((t, IDX_DIM), BF16),
            jax.ShapeDtypeStruct((batch, IDX_HEADS, seq), F32),
        ],
        scratch_shapes=[
            pltpu.VMEM((tm, D_MODEL), BF16),
            pltpu.VMEM((3, tm, LANES), F32),
            pltpu.VMEM((3, tm, LANES), F32),
        ],
        compiler_params=pltpu.CompilerParams(
            dimension_semantics=("arbitrary", "arbitrary"), vmem_limit_bytes=VMEM_LIMIT),
        name="attn_in",
    )(x2d, pos2d, g, w_main, w_kw, q_g, k_g, inv_q, inv_i)


def _fold8(x, op):
    rows, cols = x.shape
    x = x.reshape(rows // SUBLANES, SUBLANES, cols)
    if op is jnp.add:
        return x.sum(axis=0)
    return x.min(axis=0) if op is jnp.minimum else x.max(axis=0)


def _dsa_kernel(qit_ref, wit_ref, ki_ref, qg_ref, k_ref, vt_ref, o_ref, sc_ref, bias_ref,
                m_ref, acc_ref, *, tq, kc, top_k, idx_bits):
    i = pl.program_id(1)
    t0 = i * tq
    n_kc = (t0 + tq + kc - 1) // kc
    row = lax.broadcasted_iota(jnp.int32, (kc, tq), 0)
    t_idx = t0 + lax.broadcasted_iota(jnp.int32, (kc, tq), 1)

    def score_chunk(c, carry):
        lo8, hi8 = carry
        k_rows = ki_ref[0, pl.ds(pl.multiple_of(c * kc, kc), kc), :]
        acc = jnp.zeros((kc, tq), F32)
        for h in range(IDX_HEADS):
            rel = jnp.dot(k_rows, qit_ref[0, h * IDX_DIM:(h + 1) * IDX_DIM, :],
                          preferred_element_type=F32)
            acc = acc + wit_ref[0, h:h + 1, :] * jnp.maximum(rel, 0.0)
        causal = c * kc + row <= t_idx
        sc_ref[c] = jnp.where(causal, acc, -jnp.inf)
        lo8 = jnp.minimum(lo8, _fold8(jnp.where(causal, acc, jnp.inf), jnp.minimum))
        hi8 = jnp.maximum(hi8, _fold8(jnp.where(causal, acc, -jnp.inf), jnp.maximum))
        return lo8, hi8

    lo8, hi8 = lax.fori_loop(0, n_kc, score_chunk, (jnp.full((SUBLANES, tq), jnp.inf, F32),
                                                     jnp.full((SUBLANES, tq), -jnp.inf, F32)))
    row_min = jnp.min(lo8, axis=0, keepdims=True)
    row_max = jnp.max(hi8, axis=0, keepdims=True)

    def count(pred_fn):
        def body(c, cnt):
            return cnt + _fold8(pred_fn(c, sc_ref[c]).astype(jnp.int32), jnp.add)
        part = lax.fori_loop(0, n_kc, body, jnp.zeros((SUBLANES, tq), jnp.int32))
        return part.sum(axis=0, keepdims=True)

    n_causal = t0 + 1 + lax.broadcasted_iota(jnp.int32, (1, tq), 1)
    short = n_causal <= top_k
    max_ties = count(lambda c, sc: sc >= row_max) >= top_k
    thr0 = jnp.where(short, row_min, row_max)
    done0 = (short | max_ties).astype(jnp.int32)

    def bisect_cond(st):
        it, _, _, _, done = st
        return (it < MAX_BISECT) & (jnp.min(done) == 0)

    def bisect_body(st):
        it, lo, hi, thr, done = st
        mid = 0.5 * lo + 0.5 * hi
        inside = (mid > lo) & (mid < hi)
        cnt = count(lambda c, sc: sc >= mid)
        active = done == 0
        hit = active & inside & (cnt == top_k)
        converged = active & ~inside
        thr = jnp.where(hit, mid, jnp.where(converged, lo, thr))
        done = jnp.where(hit | converged, 1, done)
        lo = jnp.where(active & inside & (cnt > top_k), mid, lo)
        hi = jnp.where(active & inside & (cnt < top_k), mid, hi)
        return it + 1, lo, hi, thr, done

    _, _, _, thr, done = lax.while_loop(bisect_cond, bisect_body,
                                        (jnp.int32(0), row_min, row_max, thr0, done0))

    def to_key(x):
        bits = pltpu.bitcast(x, jnp.int32)
        return bits ^ ((bits >> 31) & 0x7FFFFFFF)

    def exact_kth(_):
        lo = jnp.where(count(lambda c, sc: to_key(sc) >= 0) >= top_k, 0, INT_MIN).astype(jnp.int32)

        def bit_body(b, lo):
            cand = lo + jnp.left_shift(jnp.int32(1), 30 - b)
            return jnp.where(count(lambda c, sc: to_key(sc) >= cand) >= top_k, cand, lo)

        kth = lax.fori_loop(0, 31, bit_body, lo)
        kth = pltpu.bitcast(kth ^ ((kth >> 31) & 0x7FFFFFFF), F32)
        return jnp.where(done == 0, kth, thr)

    thr = lax.cond(jnp.min(done) == 0, exact_kth, lambda _: thr, 0)

    def bias_chunk(c, cnt):
        sel = sc_ref[c] >= thr
        bias_ref[c] = jnp.where(sel, 0.0, MASKED)
        return cnt + _fold8(sel.astype(jnp.int32), jnp.add)

    n_sel = lax.fori_loop(0, n_kc, bias_chunk, jnp.zeros((SUBLANES, tq), jnp.int32))
    n_sel = n_sel.sum(axis=0, keepdims=True)

    @pl.when(jnp.max(n_sel) > top_k)
    def _():
        need = top_k - count(lambda c, sc: sc > thr)

        def cut_body(b, p):
            cand = p + jnp.left_shift(jnp.int32(1), idx_bits - 1 - b)
            below = count(lambda c, sc: (sc == thr) & (c * kc + row < cand))
            return jnp.where(below < need, cand, p)

        cut = lax.fori_loop(0, idx_bits, cut_body, jnp.zeros((1, tq), jnp.int32))

        def rebias_chunk(c, carry):
            sc = sc_ref[c]
            sel = (sc > thr) | ((sc == thr) & (c * kc + row <= cut))
            bias_ref[c] = jnp.where(sel, 0.0, MASKED)
            return carry

        lax.fori_loop(0, n_kc, rebias_chunk, 0)

    m_ref[...] = jnp.full(m_ref.shape, MASKED, F32)
    acc_ref[...] = jnp.zeros(acc_ref.shape, F32)

    def attn_chunk(c, carry):
        bias = bias_ref[c]
        rows = pl.ds(pl.multiple_of(c * kc, kc), kc)
        for n in range(N_KV_HEADS):
            k_c = k_ref[0, rows, n * HEAD_DIM:(n + 1) * HEAD_DIM]
            s_all = jnp.dot(k_c, qg_ref[0, n], preferred_element_type=F32)
            ps, alphas = [], []
            for g in range(GROUP):
                h = n * GROUP + g
                s = s_all[:, g * tq:(g + 1) * tq] + bias
                m_old = m_ref[h]
                m_new = jnp.maximum(m_old, jnp.max(s, axis=0, keepdims=True))
                ps.append(jnp.exp2((s - m_new).astype(BF16)))
                alphas.append(jnp.exp2(m_old - m_new))
                m_ref[h] = m_new
            p = jnp.concatenate(ps, axis=1)
            alpha = jnp.concatenate(alphas, axis=1)
            acc_ref[n] = alpha * acc_ref[n] + jnp.dot(
                vt_ref[0, c, n], p, preferred_element_type=F32)
        return carry

    lax.fori_loop(0, n_kc, attn_chunk, 0)
    for h in range(N_HEADS):
        n, g = divmod(h, GROUP)
        cols = slice(g * tq, (g + 1) * tq)
        o_t = acc_ref[n, :HEAD_DIM, cols] / acc_ref[n, HEAD_DIM:HEAD_DIM + 1, cols]
        o_ref[0, :, h * HEAD_DIM:(h + 1) * HEAD_DIM] = o_t.T.astype(BF16)


def _dsa(qit, wit, ki, qg, k, vt, *, tq, kc):
    b, s, _ = k.shape
    top_k = min(TOPK_MAX, s // 4)
    n_chunks = s // kc
    n_qb = s // tq
    v_rows = vt.shape[3]
    per_batch = pl.Buffered(1)
    return pl.pallas_call(
        functools.partial(_dsa_kernel, tq=tq, kc=kc, top_k=top_k, idx_bits=(s - 1).bit_length()),
        grid=(b, s // tq),
        in_specs=[
            pl.BlockSpec((1, IDX_Q_W, tq), lambda bb, i: (bb, 0, i)),
            pl.BlockSpec((1, IDX_HEADS, tq), lambda bb, i: (bb, 0, i)),
            pl.BlockSpec((1, s, IDX_DIM), lambda bb, i: (bb, 0, 0), pipeline_mode=per_batch),
            pl.BlockSpec((1, N_KV_HEADS, HEAD_DIM, GROUP * tq), lambda bb, i: (bb * n_qb + i, 0, 0, 0)),
            pl.BlockSpec((1, s, KV_W), lambda bb, i: (bb, 0, 0), pipeline_mode=per_batch),
            pl.BlockSpec((1, n_chunks, N_KV_HEADS, v_rows, kc), lambda bb, i: (bb, 0, 0, 0, 0),
                         pipeline_mode=per_batch),
        ],
        out_specs=pl.BlockSpec((1, tq, Q_W), lambda bb, i: (bb, i, 0)),
        out_shape=jax.ShapeDtypeStruct((b, s, Q_W), BF16),
        scratch_shapes=[
            pltpu.VMEM((n_chunks, kc, tq), F32),
            pltpu.VMEM((n_chunks, kc, tq), F32),
            pltpu.VMEM((N_HEADS, 1, tq), F32),
            pltpu.VMEM((N_KV_HEADS, v_rows, GROUP * tq), F32),
        ],
        compiler_params=pltpu.CompilerParams(
            dimension_semantics=("arbitrary", "arbitrary"), vmem_limit_bytes=VMEM_LIMIT),
        name="dsa",
    )(qit, wit, ki, qg, k, vt)


def _out_proj_kernel(a_ref, w_ref, x_ref, o_ref):
    o_ref[...] = x_ref[...] + jnp.dot(a_ref[...], w_ref[...], preferred_element_type=F32)


def _out_proj(a2d, w, x2d, *, tm=1024, tn=512):
    t, k = a2d.shape
    n = w.shape[1]
    return pl.pallas_call(
        _out_proj_kernel,
        grid=(t // tm, n // tn),
        in_specs=[
            pl.BlockSpec((tm, k), lambda i, j: (i, 0)),
            pl.BlockSpec((k, tn), lambda i, j: (0, j)),
            pl.BlockSpec((tm, tn), lambda i, j: (i, j)),
        ],
        out_specs=pl.BlockSpec((tm, tn), lambda i, j: (i, j)),
        out_shape=jax.ShapeDtypeStruct((t, n), F32),
        compiler_params=pltpu.CompilerParams(
            dimension_semantics=("arbitrary", "arbitrary"), vmem_limit_bytes=VMEM_LIMIT),
        name="out_proj",
    )(a2d, w, x2d)


def _conv_kernel(x_ref, g_ref, wb_ref, wc_ref, wu_ref, cw_ref, wo_ref, o_ref, h_ref, carry_ref,
                 *, tm, tiles_per_seq):
    i = pl.program_id(0)
    j = pl.program_id(1)

    @pl.when(j == 0)
    def _():
        x = x_ref[...]
        h_ref[...] = _rmsnorm_rows(x, g_ref[...]).astype(BF16)
        o_ref[...] = x

    h = h_ref[...]
    b_gate = jnp.dot(h, wb_ref[...], preferred_element_type=F32)
    c_gate = jnp.dot(h, wc_ref[...], preferred_element_type=F32)
    u = jnp.dot(h, wu_ref[...], preferred_element_type=F32)
    z = c_gate * u
    prev = jnp.where(i % tiles_per_seq == 0, 0.0, carry_ref[j])
    carry_ref[j] = z[tm - SUBLANES:, :]
    r = lax.broadcasted_iota(jnp.int32, z.shape, 0)
    z1 = jnp.where(r == 0, prev[7:8, :], pltpu.roll(z, 1, 0))
    z2 = jnp.where(r == 0, prev[6:7, :], jnp.where(r == 1, prev[7:8, :], pltpu.roll(z, 2, 0)))
    y = z2 * cw_ref[0:1, :] + z1 * cw_ref[1:2, :] + z * cw_ref[2:3, :]
    gated = (b_gate * y).astype(BF16)
    o_ref[...] += jnp.dot(gated, wo_ref[...], preferred_element_type=F32)


def _conv_mixer(x2d, g, w_in, cw, wo, *, seq, tm=512, tn=512):
    t = x2d.shape[0]
    n_chunks = D_MODEL // tn
    return pl.pallas_call(
        functools.partial(_conv_kernel, tm=tm, tiles_per_seq=seq // tm),
        grid=(t // tm, n_chunks),
        in_specs=[
            pl.BlockSpec((tm, D_MODEL), lambda i, j: (i, 0)),
            pl.BlockSpec((1, D_MODEL), lambda i, j: (0, 0)),
            pl.BlockSpec((D_MODEL, tn), lambda i, j: (0, j)),
            pl.BlockSpec((D_MODEL, tn), lambda i, j: (0, j + n_chunks)),
            pl.BlockSpec((D_MODEL, tn), lambda i, j: (0, j + 2 * n_chunks)),
            pl.BlockSpec((3, tn), lambda i, j: (0, j)),
            pl.BlockSpec((tn, D_MODEL), lambda i, j: (j, 0)),
        ],
        out_specs=pl.BlockSpec((tm, D_MODEL), lambda i, j: (i, 0)),
        out_shape=jax.ShapeDtypeStruct((t, D_MODEL), F32),
        scratch_shapes=[
            pltpu.VMEM((tm, D_MODEL), BF16),
            pltpu.VMEM((n_chunks, SUBLANES, tn), F32),
        ],
        compiler_params=pltpu.CompilerParams(
            dimension_semantics=("arbitrary", "arbitrary"), vmem_limit_bytes=VMEM_LIMIT),
        name="conv_mixer",
    )(x2d, g, w_in, w_in, w_in, cw, wo)


def _mlp_kernel(x_ref, g_ref, w1_ref, w2_ref, o_ref, h_ref):
    @pl.when(pl.program_id(1) == 0)
    def _():
        x = x_ref[...]
        h_ref[...] = _rmsnorm_rows(x, g_ref[...]).astype(BF16)
        o_ref[...] = x

    a = jnp.maximum(jnp.dot(h_ref[...], w1_ref[...], preferred_element_type=F32), 0.0)
    o_ref[...] += jnp.dot((a * a).astype(BF16), w2_ref[...], preferred_element_type=F32)


def _mlp(x2d, g, w1, w2, *, tm=512, tf=1024):
    t = x2d.shape[0]
    return pl.pallas_call(
        _mlp_kernel,
        grid=(t // tm, D_FF // tf),
        in_specs=[
            pl.BlockSpec((tm, D_MODEL), lambda i, j: (i, 0)),
            pl.BlockSpec((1, D_MODEL), lambda i, j: (0, 0)),
            pl.BlockSpec((D_MODEL, tf), lambda i, j: (0, j)),
            pl.BlockSpec((tf, D_MODEL), lambda i, j: (j, 0)),
        ],
        out_specs=pl.BlockSpec((tm, D_MODEL), lambda i, j: (i, 0)),
        out_shape=jax.ShapeDtypeStruct((t, D_MODEL), F32),
        scratch_shapes=[pltpu.VMEM((tm, D_MODEL), BF16)],
        compiler_params=pltpu.CompilerParams(
            dimension_semantics=("arbitrary", "arbitrary"), vmem_limit_bytes=VMEM_LIMIT),
        name="mlp",
    )(x2d, g, w1, w2)


def _lane_inv_freq(rot_dim):
    inv = ROPE_THETA ** (-jnp.arange(0, rot_dim, 2, dtype=F32) / rot_dim)
    return jnp.tile(inv, LANES // inv.shape[0])[None, :]


def _dsa_layer(x2d, pos2d, batch, seq, g, w_in, q_g, k_g, w_out, *, tq=512, kc=512):
    w_main = w_in[:, :MAIN_W].astype(BF16)
    w_kw = jnp.pad(w_in[:, MAIN_W:], ((0, 0), (0, LANES - (IDX_DIM + IDX_HEADS)))).astype(BF16)
    qg, k, vt, qit, ki, wit = _attn_in(
        x2d, pos2d, g[None, :], w_main, w_kw, q_g[None, :], k_g[None, :],
        _lane_inv_freq(ROPE_DIM), _lane_inv_freq(IDX_ROPE_DIM), batch=batch, seq=seq, tq=tq, kc=kc)
    o = _dsa(qit, wit, ki.reshape(batch, seq, IDX_DIM), qg, k.reshape(batch, seq, KV_W),
             vt.reshape((batch, seq // kc) + vt.shape[1:]), tq=tq, kc=kc)
    return _out_proj(o.reshape(batch * seq, Q_W), w_out.astype(BF16), x2d)


def kernel(x, positions, attn_norm_g, attn_w_in, attn_q_norm_g, attn_k_norm_g, attn_w_out,
           conv_norm_g, conv_w_in, conv_w, conv_w_out, mlp_norm_g, mlp_w1, mlp_w2):
    batch, seq, d = x.shape
    depth = mlp_w1.shape[0]
    x2d = x.reshape(batch * seq, d)
    pos2d = positions.reshape(batch * seq, 1)
    for i in range(depth):
        j = i // 2
        if i % 2 == 0:
            x2d = _dsa_layer(x2d, pos2d, batch, seq, attn_norm_g[j], attn_w_in[j],
                             attn_q_norm_g[j], attn_k_norm_g[j], attn_w_out[j])
        else:
            x2d = _conv_mixer(x2d, conv_norm_g[j][None, :], conv_w_in[j].astype(BF16), conv_w[j],
                              conv_w_out[j].astype(BF16), seq=seq)
        x2d = _mlp(x2d, mlp_norm_g[i][None, :], mlp_w1[i].astype(BF16), mlp_w2[i].astype(BF16))
    return x2d.reshape(batch, seq, d)
```

```python
import functools
import math

import jax
import jax.numpy as jnp
from jax import lax
from jax.experimental import pallas as pl
from jax.experimental.pallas import tpu as pltpu

D_MODEL = 2048
N_HEADS = 16
N_KV_HEADS = 4
HEAD_DIM = 128
GROUP = N_HEADS // N_KV_HEADS
ROPE_DIM = HEAD_DIM // 4
ROPE_THETA = 500000.0
IDX_HEADS = 16
IDX_DIM = 64
IDX_ROPE_DIM = IDX_DIM // 4
TOPK_MAX = 256
Q_W = N_HEADS * HEAD_DIM
KV_W = N_KV_HEADS * HEAD_DIM
IDX_Q_W = IDX_HEADS * IDX_DIM
MAIN_W = Q_W + 2 * KV_W + IDX_Q_W
D_FF = 4 * D_MODEL
EPS = 1e-6
LANES = 128
SUBLANES = 8
VMEM_LIMIT = 56 * 1024 * 1024

INT_MIN = -(2 ** 31)
MASKED = -1e30
MAX_BISECT = 40
Q_SCALE_LOG2E = HEAD_DIM ** -0.5 * math.log2(math.e)

F32 = jnp.float32
BF16 = jnp.bfloat16


def _rmsnorm_rows(x, g):
    ms = jnp.mean(x * x, axis=-1, keepdims=True)
    return (x * lax.rsqrt(ms + EPS)) * g


def _rope_tables(cos, sin, period, half):
    d = lax.broadcasted_iota(jnp.int32, cos.shape, 1) & (period - 1)
    c = jnp.where(d < 2 * half, cos, 1.0)
    s_lo = jnp.where(d < half, -sin, 0.0)
    s_hi = jnp.where((d >= half) & (d < 2 * half), sin, 0.0)
    return c, s_lo, s_hi


def _rope(x, c, s_lo, s_hi, half):
    return x * c + pltpu.roll(x, LANES - half, 1) * s_lo + pltpu.roll(x, half, 1) * s_hi


V_ONES_ROWS = 2 * SUBLANES
IDX_LANE_SHIFT = ROPE_DIM


def _attn_in_kernel(x_ref, pos_ref, g_ref, w_ref, wkw_ref, qgain_ref, kgain_ref, inv_ref,
                    qg_ref, k_ref, vt_ref, qit_ref, ki_ref, wit_ref, h_ref, acc_ref, tq_ref, ti_ref,
                    *, tn, tq, kc):
    j = pl.program_id(1)
    tm = x_ref.shape[0]
    n_q = Q_W // tn
    n_tiles = MAIN_W // tn

    def project(slot):
        acc_ref[slot] = jnp.dot(h_ref[...], w_ref[...], preferred_element_type=F32)

    def prologue():
        h = _rmsnorm_rows(x_ref[...], g_ref[...]).astype(BF16)
        h_ref[...] = h
        ang = pos_ref[...].astype(F32) * inv_ref[...]
        cos, sin = jnp.cos(ang), jnp.sin(ang)
        cq, sq_lo, sq_hi = _rope_tables(cos, sin, HEAD_DIM, ROPE_DIM // 2)
        tq_ref[0], tq_ref[1], tq_ref[2] = cq, sq_lo, sq_hi
        back = LANES - IDX_LANE_SHIFT
        ci, si_lo, si_hi = _rope_tables(pltpu.roll(cos, back, 1), pltpu.roll(sin, back, 1),
                                        IDX_DIM, IDX_ROPE_DIM // 2)
        ti_ref[0], ti_ref[1], ti_ref[2] = ci, si_lo, si_hi
        kw = jnp.dot(h, wkw_ref[...], preferred_element_type=F32)
        roped = _rope(kw, ci, si_lo, si_hi, IDX_ROPE_DIM // 2)
        ki_ref[...] = roped[:, :IDX_DIM].astype(BF16)
        scale = IDX_HEADS ** -0.5 * IDX_DIM ** -0.5
        wit_ref[0] = (kw * scale).T[IDX_DIM:IDX_DIM + IDX_HEADS, :]

    def head_norm_rope(acc, hh, gain_ref):
        y = _rmsnorm_rows(acc[:, hh * HEAD_DIM:(hh + 1) * HEAD_DIM], gain_ref[...])
        return _rope(y, tq_ref[0], tq_ref[1], tq_ref[2], ROPE_DIM // 2)

    def finish(tile, slot):
        acc = acc_ref[slot]
        if tile < n_q:
            for g in range(GROUP):
                y = head_norm_rope(acc, g, qgain_ref) * Q_SCALE_LOG2E
                for r in range(tm // tq):
                    qg_ref[r, 0, :, g * tq:(g + 1) * tq] = y[r * tq:(r + 1) * tq, :].T.astype(BF16)
        elif tile == n_q:
            for hh in range(N_KV_HEADS):
                k_ref[:, hh * HEAD_DIM:(hh + 1) * HEAD_DIM] = (
                    head_norm_rope(acc, hh, kgain_ref).astype(BF16))
        elif tile == n_q + 1:
            for n in range(N_KV_HEADS):
                for cc in range(tm // kc):
                    v = acc[cc * kc:(cc + 1) * kc, n * HEAD_DIM:(n + 1) * HEAD_DIM]
                    vt_ref[cc, n, :HEAD_DIM, :] = v.T.astype(BF16)
                    vt_ref[cc, n, HEAD_DIM:, :] = jnp.ones((V_ONES_ROWS, kc), BF16)
        else:
            for hh in range(tn // LANES):
                sl = slice(hh * LANES, (hh + 1) * LANES)
                y = _rope(acc[:, sl], ti_ref[0], ti_ref[1], ti_ref[2], IDX_ROPE_DIM // 2)
                qit_ref[0, sl, :] = y.T.astype(BF16)

    for step in range(n_tiles + 1):
        @pl.when(j == step)
        def _(step=step):
            if step == 0:
                prologue()
            if step < n_tiles:
                project(step % 2)
            if step > 0:
                finish(step - 1, (step - 1) % 2)


def _attn_in(x2d, pos2d, g, w_in, w_kw, q_g, k_g, inv, *, layer, batch, seq, tq, kc, tm=1024, tn=512):
    t = x2d.shape[0]
    assert tn == KV_W == GROUP * HEAD_DIM and tm % tq == 0 and tm % kc == 0 and seq % tm == 0
    tiles_per_seq = seq // tm
    n_q = Q_W // tn
    n_tiles = MAIN_W // tn
    first_qi = n_tiles - IDX_Q_W // tn
    return pl.pallas_call(
        functools.partial(_attn_in_kernel, tn=tn, tq=tq, kc=kc),
        grid=(t // tm, n_tiles + 1),
        in_specs=[
            pl.BlockSpec((tm, D_MODEL), lambda i, j: (i, 0)),
            pl.BlockSpec((tm, 1), lambda i, j: (i, 0)),
            pl.BlockSpec((1, D_MODEL), lambda i, j: (0, 0)),
            pl.BlockSpec((None, D_MODEL, tn), lambda i, j: (layer, 0, jnp.minimum(j, n_tiles - 1))),
            pl.BlockSpec((None, D_MODEL, LANES), lambda i, j: (layer, 0, 0)),
            pl.BlockSpec((1, HEAD_DIM), lambda i, j: (0, 0)),
            pl.BlockSpec((1, HEAD_DIM), lambda i, j: (0, 0)),
            pl.BlockSpec((1, LANES), lambda i, j: (0, 0)),
        ],
        out_specs=[
            pl.BlockSpec((tm // tq, 1, HEAD_DIM, GROUP * tq),
                         lambda i, j: (i, jnp.clip(j - 1, 0, n_q - 1), 0, 0)),
            pl.BlockSpec((tm, KV_W), lambda i, j: (i, 0)),
            pl.BlockSpec((tm // kc, N_KV_HEADS, HEAD_DIM + V_ONES_ROWS, kc), lambda i, j: (i, 0, 0, 0)),
            pl.BlockSpec((1, tn, tm),
                         lambda i, j: (i // tiles_per_seq, jnp.maximum(j - 1 - first_qi, 0),
                                       i % tiles_per_seq)),
            pl.BlockSpec((tm, IDX_DIM), lambda i, j: (i, 0)),
            pl.BlockSpec((1, IDX_HEADS, tm), lambda i, j: (i // tiles_per_seq, 0, i % tiles_per_seq)),
        ],
        out_shape=[
            jax.ShapeDtypeStruct((t // tq, N_KV_HEADS, HEAD_DIM, GROUP * tq), BF16),
            jax.ShapeDtypeStruct((t, KV_W), BF16),
            jax.ShapeDtypeStruct((t // kc, N_KV_HEADS, HEAD_DIM + V_ONES_ROWS, kc), BF16),
            jax.ShapeDtypeStruct((batch, IDX_Q_W, seq), BF16),
            jax.ShapeDtypeStruct((t, IDX_DIM), BF16),
            jax.ShapeDtypeStruct((batch, IDX_HEADS, seq), F32),
        ],
        scratch_shapes=[
            pltpu.VMEM((tm, D_MODEL), BF16),
            pltpu.VMEM((2, tm, tn), F32),
            pltpu.VMEM((3, tm, LANES), F32),
            pltpu.VMEM((3, tm, LANES), F32),
        ],
        compiler_params=pltpu.CompilerParams(
            dimension_semantics=("arbitrary", "arbitrary"), vmem_limit_bytes=VMEM_LIMIT),
        name="attn_in",
    )(x2d, pos2d, g, w_in, w_kw, q_g, k_g, inv)


def _fold8(x, op):
    rows, cols = x.shape
    x = x.reshape(rows // SUBLANES, SUBLANES, cols)
    if op is jnp.add:
        return x.sum(axis=0)
    return x.min(axis=0) if op is jnp.minimum else x.max(axis=0)


def _dsa_kernel(qit_ref, wit_ref, ki_ref, qg_ref, k_ref, vt_ref, o_ref, sc_ref, bias_ref,
                m_ref, acc_ref, *, tq, kc, top_k, idx_bits):
    i = pl.program_id(1)
    t0 = i * tq
    n_kc = (t0 + tq + kc - 1) // kc
    row = lax.broadcasted_iota(jnp.int32, (kc, tq), 0)
    t_idx = t0 + lax.broadcasted_iota(jnp.int32, (kc, tq), 1)

    def score_chunk(c, carry):
        lo8, hi8 = carry
        k_rows = ki_ref[0, pl.ds(pl.multiple_of(c * kc, kc), kc), :]
        acc = jnp.zeros((kc, tq), F32)
        for h in range(IDX_HEADS):
            rel = jnp.dot(k_rows, qit_ref[0, h * IDX_DIM:(h + 1) * IDX_DIM, :],
                          preferred_element_type=F32)
            acc = acc + wit_ref[0, h:h + 1, :] * jnp.maximum(rel, 0.0)
        causal = c * kc + row <= t_idx
        sc_ref[c] = jnp.where(causal, acc, -jnp.inf)
        lo8 = jnp.minimum(lo8, _fold8(jnp.where(causal, acc, jnp.inf), jnp.minimum))
        hi8 = jnp.maximum(hi8, _fold8(jnp.where(causal, acc, -jnp.inf), jnp.maximum))
        return lo8, hi8

    lo8, hi8 = lax.fori_loop(0, n_kc, score_chunk, (jnp.full((SUBLANES, tq), jnp.inf, F32),
                                                     jnp.full((SUBLANES, tq), -jnp.inf, F32)))
    row_min = jnp.min(lo8, axis=0, keepdims=True)
    row_max = jnp.max(hi8, axis=0, keepdims=True)

    def count(pred_fn):
        def body(c, cnt):
            return cnt + _fold8(pred_fn(c, sc_ref[c]).astype(jnp.int32), jnp.add)
        part = lax.fori_loop(0, n_kc, body, jnp.zeros((SUBLANES, tq), jnp.int32))
        return part.sum(axis=0, keepdims=True)

    n_causal = t0 + 1 + lax.broadcasted_iota(jnp.int32, (1, tq), 1)
    short = n_causal <= top_k
    max_ties = count(lambda c, sc: sc >= row_max) >= top_k
    thr0 = jnp.where(short, row_min, row_max)
    done0 = (short | max_ties).astype(jnp.int32)

    def bisect_cond(st):
        it, _, _, _, done = st
        return (it < MAX_BISECT) & (jnp.min(done) == 0)

    def bisect_body(st):
        it, lo, hi, thr, done = st
        mid = 0.5 * lo + 0.5 * hi
        inside = (mid > lo) & (mid < hi)
        cnt = count(lambda c, sc: sc >= mid)
        active = done == 0
        hit = active & inside & (cnt == top_k)
        converged = active & ~inside
        thr = jnp.where(hit, mid, jnp.where(converged, lo, thr))
        done = jnp.where(hit | converged, 1, done)
        lo = jnp.where(active & inside & (cnt > top_k), mid, lo)
        hi = jnp.where(active & inside & (cnt < top_k), mid, hi)
        return it + 1, lo, hi, thr, done

    _, _, _, thr, done = lax.while_loop(bisect_cond, bisect_body,
                                        (jnp.int32(0), row_min, row_max, thr0, done0))

    def to_key(x):
        bits = pltpu.bitcast(x, jnp.int32)
        return bits ^ ((bits >> 31) & 0x7FFFFFFF)

    def exact_kth(_):
        lo = jnp.where(count(lambda c, sc: to_key(sc) >= 0) >= top_k, 0, INT_MIN).astype(jnp.int32)

        def bit_body(b, lo):
            cand = lo + jnp.left_shift(jnp.int32(1), 30 - b)
            return jnp.where(count(lambda c, sc: to_key(sc) >= cand) >= top_k, cand, lo)

        kth = lax.fori_loop(0, 31, bit_body, lo)
        kth = pltpu.bitcast(kth ^ ((kth >> 31) & 0x7FFFFFFF), F32)
        return jnp.where(done == 0, kth, thr)

    thr = lax.cond(jnp.min(done) == 0, exact_kth, lambda _: thr, 0)

    def bias_chunk(c, cnt):
        sel = sc_ref[c] >= thr
        bias_ref[c] = jnp.where(sel, 0.0, MASKED)
        return cnt + _fold8(sel.astype(jnp.int32), jnp.add)

    n_sel = lax.fori_loop(0, n_kc, bias_chunk, jnp.zeros((SUBLANES, tq), jnp.int32))
    n_sel = n_sel.sum(axis=0, keepdims=True)

    @pl.when(jnp.max(n_sel) > top_k)
    def _():
        need = top_k - count(lambda c, sc: sc > thr)

        def cut_body(b, p):
            cand = p + jnp.left_shift(jnp.int32(1), idx_bits - 1 - b)
            below = count(lambda c, sc: (sc == thr) & (c * kc + row < cand))
            return jnp.where(below < need, cand, p)

        cut = lax.fori_loop(0, idx_bits, cut_body, jnp.zeros((1, tq), jnp.int32))

        def rebias_chunk(c, carry):
            sc = sc_ref[c]
            sel = (sc > thr) | ((sc == thr) & (c * kc + row <= cut))
            bias_ref[c] = jnp.where(sel, 0.0, MASKED)
            return carry

        lax.fori_loop(0, n_kc, rebias_chunk, 0)

    m_ref[...] = jnp.full(m_ref.shape, MASKED, F32)
    acc_ref[...] = jnp.zeros(acc_ref.shape, F32)

    def attn_chunk(c, carry):
        bias = bias_ref[c]
        rows = pl.ds(pl.multiple_of(c * kc, kc), kc)
        for n in range(N_KV_HEADS):
            k_c = k_ref[0, rows, n * HEAD_DIM:(n + 1) * HEAD_DIM]
            s_all = jnp.dot(k_c, qg_ref[0, n], preferred_element_type=F32)
            ps, alphas = [], []
            for g in range(GROUP):
                h = n * GROUP + g
                s = s_all[:, g * tq:(g + 1) * tq] + bias
                m_old = m_ref[h]
                m_new = jnp.maximum(m_old, jnp.max(s, axis=0, keepdims=True))
                ps.append(jnp.exp2((s - m_new).astype(BF16)))
                alphas.append(jnp.exp2(m_old - m_new))
                m_ref[h] = m_new
            p = jnp.concatenate(ps, axis=1)
            alpha = jnp.concatenate(alphas, axis=1)
            acc_ref[n] = alpha * acc_ref[n] + jnp.dot(
                vt_ref[0, c, n], p, preferred_element_type=F32)
        return carry

    lax.fori_loop(0, n_kc, attn_chunk, 0)
    for h in range(N_HEADS):
        n, g = divmod(h, GROUP)
        cols = slice(g * tq, (g + 1) * tq)
        o_t = acc_ref[n, :HEAD_DIM, cols] / acc_ref[n, HEAD_DIM:HEAD_DIM + 1, cols]
        o_ref[0, :, h * HEAD_DIM:(h + 1) * HEAD_DIM] = o_t.T.astype(BF16)


def _dsa(qit, wit, ki, qg, k, vt, *, tq, kc):
    b, s, _ = k.shape
    top_k = min(TOPK_MAX, s // 4)
    n_chunks = s // kc
    n_qb = s // tq
    v_rows = vt.shape[3]
    per_batch = pl.Buffered(1)
    return pl.pallas_call(
        functools.partial(_dsa_kernel, tq=tq, kc=kc, top_k=top_k, idx_bits=(s - 1).bit_length()),
        grid=(b, s // tq),
        in_specs=[
            pl.BlockSpec((1, IDX_Q_W, tq), lambda bb, i: (bb, 0, i)),
            pl.BlockSpec((1, IDX_HEADS, tq), lambda bb, i: (bb, 0, i)),
            pl.BlockSpec((1, s, IDX_DIM), lambda bb, i: (bb, 0, 0), pipeline_mode=per_batch),
            pl.BlockSpec((1, N_KV_HEADS, HEAD_DIM, GROUP * tq), lambda bb, i: (bb * n_qb + i, 0, 0, 0)),
            pl.BlockSpec((1, s, KV_W), lambda bb, i: (bb, 0, 0), pipeline_mode=per_batch),
            pl.BlockSpec((1, n_chunks, N_KV_HEADS, v_rows, kc), lambda bb, i: (bb, 0, 0, 0, 0),
                         pipeline_mode=per_batch),
        ],
        out_specs=pl.BlockSpec((1, tq, Q_W), lambda bb, i: (bb, i, 0)),
        out_shape=jax.ShapeDtypeStruct((b, s, Q_W), BF16),
        scratch_shapes=[
            pltpu.VMEM((n_chunks, kc, tq), F32),
            pltpu.VMEM((n_chunks, kc, tq), F32),
            pltpu.VMEM((N_HEADS, 1, tq), F32),
            pltpu.VMEM((N_KV_HEADS, v_rows, GROUP * tq), F32),
        ],
        compiler_params=pltpu.CompilerParams(
            dimension_semantics=("arbitrary", "arbitrary"), vmem_limit_bytes=VMEM_LIMIT),
        name="dsa",
    )(qit, wit, ki, qg, k, vt)


def _out_proj_kernel(a_ref, w_ref, x_ref, o_ref):
    o_ref[...] = x_ref[...] + jnp.dot(a_ref[...], w_ref[...], preferred_element_type=F32)


def _out_proj(a2d, w, x2d, *, layer, tm=1024, tn=512):
    t, k = a2d.shape
    n = w.shape[2]
    return pl.pallas_call(
        _out_proj_kernel,
        grid=(t // tm, n // tn),
        in_specs=[
            pl.BlockSpec((tm, k), lambda i, j: (i, 0)),
            pl.BlockSpec((None, k, tn), lambda i, j: (layer, 0, j)),
            pl.BlockSpec((tm, tn), lambda i, j: (i, j)),
        ],
        out_specs=pl.BlockSpec((tm, tn), lambda i, j: (i, j)),
        out_shape=jax.ShapeDtypeStruct((t, n), F32),
        compiler_params=pltpu.CompilerParams(
            dimension_semantics=("arbitrary", "arbitrary"), vmem_limit_bytes=VMEM_LIMIT),
        name="out_proj",
    )(a2d, w, x2d)


def _conv_kernel(x_ref, g_ref, wb_ref, wc_ref, wu_ref, cw_ref, wo_ref, o_ref, h_ref, carry_ref,
                 *, tm, tiles_per_seq):
    i = pl.program_id(0)
    j = pl.program_id(1)

    @pl.when(j == 0)
    def _():
        x = x_ref[...]
        h_ref[...] = _rmsnorm_rows(x, g_ref[...]).astype(BF16)
        o_ref[...] = x

    h = h_ref[...]
    b_gate = jnp.dot(h, wb_ref[...], preferred_element_type=F32)
    c_gate = jnp.dot(h, wc_ref[...], preferred_element_type=F32)
    u = jnp.dot(h, wu_ref[...], preferred_element_type=F32)
    z = c_gate * u
    prev = jnp.where(i % tiles_per_seq == 0, 0.0, carry_ref[j])
    carry_ref[j] = z[tm - SUBLANES:, :]
    r = lax.broadcasted_iota(jnp.int32, z.shape, 0)
    z1 = jnp.where(r == 0, prev[7:8, :], pltpu.roll(z, 1, 0))
    z2 = jnp.where(r == 0, prev[6:7, :], jnp.where(r == 1, prev[7:8, :], pltpu.roll(z, 2, 0)))
    y = z2 * cw_ref[0:1, :] + z1 * cw_ref[1:2, :] + z * cw_ref[2:3, :]
    gated = (b_gate * y).astype(BF16)
    o_ref[...] += jnp.dot(gated, wo_ref[...], preferred_element_type=F32)


def _conv_mixer(x2d, g, w_in, cw, wo, *, layer, seq, tm=512, tn=512):
    t = x2d.shape[0]
    n_chunks = D_MODEL // tn
    return pl.pallas_call(
        functools.partial(_conv_kernel, tm=tm, tiles_per_seq=seq // tm),
        grid=(t // tm, n_chunks),
        in_specs=[
            pl.BlockSpec((tm, D_MODEL), lambda i, j: (i, 0)),
            pl.BlockSpec((1, D_MODEL), lambda i, j: (0, 0)),
            pl.BlockSpec((None, D_MODEL, tn), lambda i, j: (layer, 0, j)),
            pl.BlockSpec((None, D_MODEL, tn), lambda i, j: (layer, 0, j + n_chunks)),
            pl.BlockSpec((None, D_MODEL, tn), lambda i, j: (layer, 0, j + 2 * n_chunks)),
            pl.BlockSpec((None, 3, tn), lambda i, j: (layer, 0, j)),
            pl.BlockSpec((None, tn, D_MODEL), lambda i, j: (layer, j, 0)),
        ],
        out_specs=pl.BlockSpec((tm, D_MODEL), lambda i, j: (i, 0)),
        out_shape=jax.ShapeDtypeStruct((t, D_MODEL), F32),
        scratch_shapes=[
            pltpu.VMEM((tm, D_MODEL), BF16),
            pltpu.VMEM((n_chunks, SUBLANES, tn), F32),
        ],
        compiler_params=pltpu.CompilerParams(
            dimension_semantics=("arbitrary", "arbitrary"), vmem_limit_bytes=VMEM_LIMIT),
        name="conv_mixer",
    )(x2d, g, w_in, w_in, w_in, cw, wo)


def _mlp_kernel(x_ref, g_ref, w1_ref, w2_ref, o_ref, h_ref):
    @pl.when(pl.program_id(1) == 0)
    def _():
        x = x_ref[...]
        h_ref[...] = _rmsnorm_rows(x, g_ref[...]).astype(BF16)
        o_ref[...] = x

    a = jnp.maximum(jnp.dot(h_ref[...], w1_ref[...], preferred_element_type=F32), 0.0)
    o_ref[...] += jnp.dot((a * a).astype(BF16), w2_ref[...], preferred_element_type=F32)


def _mlp(x2d, g, w1, w2, *, layer, tm=512, tf=1024):
    t = x2d.shape[0]
    return pl.pallas_call(
        _mlp_kernel,
        grid=(t // tm, D_FF // tf),
        in_specs=[
            pl.BlockSpec((tm, D_MODEL), lambda i, j: (i, 0)),
            pl.BlockSpec((1, D_MODEL), lambda i, j: (0, 0)),
            pl.BlockSpec((None, D_MODEL, tf), lambda i, j: (layer, 0, j)),
            pl.BlockSpec((None, tf, D_MODEL), lambda i, j: (layer, j, 0)),
        ],
        out_specs=pl.BlockSpec((tm, D_MODEL), lambda i, j: (i, 0)),
        out_shape=jax.ShapeDtypeStruct((t, D_MODEL), F32),
        scratch_shapes=[pltpu.VMEM((tm, D_MODEL), BF16)],
        compiler_params=pltpu.CompilerParams(
            dimension_semantics=("arbitrary", "arbitrary"), vmem_limit_bytes=VMEM_LIMIT),
        name="mlp",
    )(x2d, g, w1, w2)


def _rope_inv_lanes():
    def inv(rot_dim):
        return ROPE_THETA ** (-jnp.arange(0, rot_dim, 2, dtype=F32) / rot_dim)
    inv_q, inv_i = jnp.tile(inv(ROPE_DIM), 2), jnp.tile(inv(IDX_ROPE_DIM), 2)
    gap = jnp.zeros((IDX_DIM - IDX_ROPE_DIM,), F32)
    tail = jnp.zeros((LANES - IDX_LANE_SHIFT - IDX_DIM - IDX_ROPE_DIM,), F32)
    lanes = jnp.concatenate([inv_q, inv_i, gap, inv_i, tail])
    assert lanes.shape == (LANES,) and IDX_LANE_SHIFT == inv_q.shape[0]
    return lanes[None, :]


def _dsa_layer(x2d, pos2d, batch, seq, g, w_in, w_kw, q_g, k_g, w_out, layer, *, tq=512, kc=512):
    qg, k, vt, qit, ki, wit = _attn_in(
        x2d, pos2d, g[None, :], w_in, w_kw, q_g[None, :], k_g[None, :], _rope_inv_lanes(),
        layer=layer, batch=batch, seq=seq, tq=tq, kc=kc)
    o = _dsa(qit, wit, ki.reshape(batch, seq, IDX_DIM), qg, k.reshape(batch, seq, KV_W),
             vt.reshape((batch, seq // kc) + vt.shape[1:]), tq=tq, kc=kc)
    return _out_proj(o.reshape(batch * seq, Q_W), w_out, x2d, layer=layer)


def kernel(x, positions, attn_norm_g, attn_w_in, attn_q_norm_g, attn_k_norm_g, attn_w_out,
           conv_norm_g, conv_w_in, conv_w, conv_w_out, mlp_norm_g, mlp_w1, mlp_w2):
    batch, seq, d = x.shape
    depth = mlp_w1.shape[0]
    x2d = x.reshape(batch * seq, d)
    pos2d = positions.reshape(batch * seq, 1)
    attn_w_in_b, attn_w_out_b = attn_w_in.astype(BF16), attn_w_out.astype(BF16)
    attn_w_kw_b = jnp.pad(attn_w_in[:, :, MAIN_W:],
                          ((0, 0), (0, 0), (0, LANES - (IDX_DIM + IDX_HEADS)))).astype(BF16)
    conv_w_in_b, conv_w_out_b = conv_w_in.astype(BF16), conv_w_out.astype(BF16)
    mlp_w1_b, mlp_w2_b = mlp_w1.astype(BF16), mlp_w2.astype(BF16)
    for i in range(depth):
        j = i // 2
        if i % 2 == 0:
            x2d = _dsa_layer(x2d, pos2d, batch, seq, attn_norm_g[j], attn_w_in_b, attn_w_kw_b,
                             attn_q_norm_g[j], attn_k_norm_g[j], attn_w_out_b, j)
        else:
            x2d = _conv_mixer(x2d, conv_norm_g[j][None, :], conv_w_in_b, conv_w, conv_w_out_b,
                              layer=j, seq=seq)
        x2d = _mlp(x2d, mlp_norm_g[i][None, :], mlp_w1_b, mlp_w2_b, layer=i)
    return x2d.reshape(batch, seq, d)
```

```python
import functools
import math

import jax
import jax.numpy as jnp
from jax import lax
from jax.experimental import pallas as pl
from jax.experimental.pallas import tpu as pltpu

D_MODEL = 2048
N_HEADS = 16
N_KV_HEADS = 4
HEAD_DIM = 128
GROUP = N_HEADS // N_KV_HEADS
ROPE_DIM = HEAD_DIM // 4
ROPE_THETA = 500000.0
IDX_HEADS = 16
IDX_DIM = 64
IDX_ROPE_DIM = IDX_DIM // 4
TOPK_MAX = 256
Q_W = N_HEADS * HEAD_DIM
KV_W = N_KV_HEADS * HEAD_DIM
IDX_Q_W = IDX_HEADS * IDX_DIM
MAIN_W = Q_W + 2 * KV_W + IDX_Q_W
D_FF = 4 * D_MODEL
EPS = 1e-6
LANES = 128
SUBLANES = 8
VMEM_LIMIT = 56 * 1024 * 1024

INT_MIN = -(2 ** 31)
MASKED = -1e30
MAX_BISECT = 40
Q_SCALE_LOG2E = HEAD_DIM ** -0.5 * math.log2(math.e)

F32 = jnp.float32
BF16 = jnp.bfloat16


def _rmsnorm_rows(x, g):
    ms = jnp.mean(x * x, axis=-1, keepdims=True)
    return (x * lax.rsqrt(ms + EPS)) * g


def _rope_tables(cos, sin, period, half):
    d = lax.broadcasted_iota(jnp.int32, cos.shape, 1) & (period - 1)
    c = jnp.where(d < 2 * half, cos, 1.0)
    s_lo = jnp.where(d < half, -sin, 0.0)
    s_hi = jnp.where((d >= half) & (d < 2 * half), sin, 0.0)
    return c, s_lo, s_hi


def _rope(x, c, s_lo, s_hi, half):
    return x * c + pltpu.roll(x, LANES - half, 1) * s_lo + pltpu.roll(x, half, 1) * s_hi


V_ONES_ROWS = 2 * SUBLANES
IDX_LANE_SHIFT = ROPE_DIM


def _attn_in_kernel(x_ref, pos_ref, g_ref, w_ref, wkw_ref, qgain_ref, kgain_ref, inv_ref,
                    qg_ref, k_ref, vt_ref, qit_ref, ki_ref, wit_ref, h_ref, acc_ref, tq_ref, ti_ref,
                    *, tn, tq, kc):
    j = pl.program_id(1)
    tm = x_ref.shape[0]
    n_q = Q_W // tn
    n_tiles = MAIN_W // tn

    def project(slot):
        acc_ref[slot] = jnp.dot(h_ref[...], w_ref[...], preferred_element_type=F32)

    def prologue():
        h = _rmsnorm_rows(x_ref[...], g_ref[...]).astype(BF16)
        h_ref[...] = h
        ang = pos_ref[...].astype(F32) * inv_ref[...]
        cos, sin = jnp.cos(ang), jnp.sin(ang)
        cq, sq_lo, sq_hi = _rope_tables(cos, sin, HEAD_DIM, ROPE_DIM // 2)
        tq_ref[0], tq_ref[1], tq_ref[2] = cq, sq_lo, sq_hi
        back = LANES - IDX_LANE_SHIFT
        ci, si_lo, si_hi = _rope_tables(pltpu.roll(cos, back, 1), pltpu.roll(sin, back, 1),
                                        IDX_DIM, IDX_ROPE_DIM // 2)
        ti_ref[0], ti_ref[1], ti_ref[2] = ci, si_lo, si_hi
        kw = jnp.dot(h, wkw_ref[...], preferred_element_type=F32)
        roped = _rope(kw, ci, si_lo, si_hi, IDX_ROPE_DIM // 2)
        ki_ref[...] = roped[:, :IDX_DIM].astype(BF16)
        scale = IDX_HEADS ** -0.5 * IDX_DIM ** -0.5
        wit_ref[0] = (kw * scale).T[IDX_DIM:IDX_DIM + IDX_HEADS, :]

    def head_norm_rope(acc, hh, gain_ref):
        y = _rmsnorm_rows(acc[:, hh * HEAD_DIM:(hh + 1) * HEAD_DIM], gain_ref[...])
        return _rope(y, tq_ref[0], tq_ref[1], tq_ref[2], ROPE_DIM // 2)

    def finish(tile, slot):
        acc = acc_ref[slot]
        if tile < n_q:
            for g in range(GROUP):
                y = head_norm_rope(acc, g, qgain_ref) * Q_SCALE_LOG2E
                for r in range(tm // tq):
                    qg_ref[r, 0, :, g * tq:(g + 1) * tq] = y[r * tq:(r + 1) * tq, :].T.astype(BF16)
        elif tile == n_q:
            for hh in range(N_KV_HEADS):
                k_ref[:, hh * HEAD_DIM:(hh + 1) * HEAD_DIM] = (
                    head_norm_rope(acc, hh, kgain_ref).astype(BF16))
        elif tile == n_q + 1:
            for n in range(N_KV_HEADS):
                for cc in range(tm // kc):
                    v = acc[cc * kc:(cc + 1) * kc, n * HEAD_DIM:(n + 1) * HEAD_DIM]
                    vt_ref[cc, n, :HEAD_DIM, :] = v.T.astype(BF16)
                    vt_ref[cc, n, HEAD_DIM:, :] = jnp.ones((V_ONES_ROWS, kc), BF16)
        else:
            for hh in range(tn // LANES):
                sl = slice(hh * LANES, (hh + 1) * LANES)
                y = _rope(acc[:, sl], ti_ref[0], ti_ref[1], ti_ref[2], IDX_ROPE_DIM // 2)
                qit_ref[0, sl, :] = y.T.astype(BF16)

    for step in range(n_tiles + 1):
        @pl.when(j == step)
        def _(step=step):
            if step == 0:
                prologue()
            if step < n_tiles:
                project(step % 2)
            if step > 0:
                finish(step - 1, (step - 1) % 2)


def _attn_in(x2d, pos2d, g, w_in, w_kw, q_g, k_g, inv, *, layer, batch, seq, tq, kc, tm=1024, tn=512):
    t = x2d.shape[0]
    assert tn == KV_W == GROUP * HEAD_DIM and tm % tq == 0 and tm % kc == 0 and seq % tm == 0
    tiles_per_seq = seq // tm
    n_q = Q_W // tn
    n_tiles = MAIN_W // tn
    first_qi = n_tiles - IDX_Q_W // tn
    return pl.pallas_call(
        functools.partial(_attn_in_kernel, tn=tn, tq=tq, kc=kc),
        grid=(t // tm, n_tiles + 1),
        in_specs=[
            pl.BlockSpec((tm, D_MODEL), lambda i, j: (i, 0)),
            pl.BlockSpec((tm, 1), lambda i, j: (i, 0)),
            pl.BlockSpec((1, D_MODEL), lambda i, j: (0, 0)),
            pl.BlockSpec((None, D_MODEL, tn), lambda i, j: (layer, 0, jnp.minimum(j, n_tiles - 1))),
            pl.BlockSpec((None, D_MODEL, LANES), lambda i, j: (layer, 0, 0)),
            pl.BlockSpec((1, HEAD_DIM), lambda i, j: (0, 0)),
            pl.BlockSpec((1, HEAD_DIM), lambda i, j: (0, 0)),
            pl.BlockSpec((1, LANES), lambda i, j: (0, 0)),
        ],
        out_specs=[
            pl.BlockSpec((tm // tq, 1, HEAD_DIM, GROUP * tq),
                         lambda i, j: (i, jnp.clip(j - 1, 0, n_q - 1), 0, 0)),
            pl.BlockSpec((tm, KV_W), lambda i, j: (i, 0)),
            pl.BlockSpec((tm // kc, N_KV_HEADS, HEAD_DIM + V_ONES_ROWS, kc), lambda i, j: (i, 0, 0, 0)),
            pl.BlockSpec((1, tn, tm),
                         lambda i, j: (i // tiles_per_seq, jnp.maximum(j - 1 - first_qi, 0),
                                       i % tiles_per_seq)),
            pl.BlockSpec((tm, IDX_DIM), lambda i, j: (i, 0)),
            pl.BlockSpec((1, IDX_HEADS, tm), lambda i, j: (i // tiles_per_seq, 0, i % tiles_per_seq)),
        ],
        out_shape=[
            jax.ShapeDtypeStruct((t // tq, N_KV_HEADS, HEAD_DIM, GROUP * tq), BF16),
            jax.ShapeDtypeStruct((t, KV_W), BF16),
            jax.ShapeDtypeStruct((t // kc, N_KV_HEADS, HEAD_DIM + V_ONES_ROWS, kc), BF16),
            jax.ShapeDtypeStruct((batch, IDX_Q_W, seq), BF16),
            jax.ShapeDtypeStruct((t, IDX_DIM), BF16),
            jax.ShapeDtypeStruct((batch, IDX_HEADS, seq), F32),
        ],
        scratch_shapes=[
            pltpu.VMEM((tm, D_MODEL), BF16),
            pltpu.VMEM((2, tm, tn), F32),
            pltpu.VMEM((3, tm, LANES), F32),
            pltpu.VMEM((3, tm, LANES), F32),
        ],
        compiler_params=pltpu.CompilerParams(
            dimension_semantics=("arbitrary", "arbitrary"), vmem_limit_bytes=VMEM_LIMIT),
        name="attn_in",
    )(x2d, pos2d, g, w_in, w_kw, q_g, k_g, inv)


def _fold8(x, op):
    rows, cols = x.shape
    x = x.reshape(rows // SUBLANES, SUBLANES, cols)
    if op is jnp.add:
        return x.sum(axis=0)
    return x.min(axis=0) if op is jnp.minimum else x.max(axis=0)


def _dsa_kernel(qit_ref, wit_ref, ki_ref, qg_ref, k_ref, vt_ref, o_ref, sc_ref, bias_ref,
                m_ref, acc_ref, *, tq, kc, top_k, idx_bits):
    i = pl.program_id(1)
    t0 = i * tq
    n_kc = (t0 + tq + kc - 1) // kc
    row = lax.broadcasted_iota(jnp.int32, (kc, tq), 0)
    t_idx = t0 + lax.broadcasted_iota(jnp.int32, (kc, tq), 1)

    def score_chunk(c, carry):
        lo8, hi8 = carry
        k_rows = ki_ref[0, pl.ds(pl.multiple_of(c * kc, kc), kc), :]
        acc = jnp.zeros((kc, tq), F32)
        for h in range(IDX_HEADS):
            rel = jnp.dot(k_rows, qit_ref[0, h * IDX_DIM:(h + 1) * IDX_DIM, :],
                          preferred_element_type=F32)
            acc = acc + wit_ref[0, h:h + 1, :] * jnp.maximum(rel, 0.0)
        causal = c * kc + row <= t_idx
        sc_ref[c] = jnp.where(causal, acc, -jnp.inf)
        lo8 = jnp.minimum(lo8, _fold8(jnp.where(causal, acc, jnp.inf), jnp.minimum))
        hi8 = jnp.maximum(hi8, _fold8(jnp.where(causal, acc, -jnp.inf), jnp.maximum))
        return lo8, hi8

    lo8, hi8 = lax.fori_loop(0, n_kc, score_chunk, (jnp.full((SUBLANES, tq), jnp.inf, F32),
                                                     jnp.full((SUBLANES, tq), -jnp.inf, F32)))
    row_min = jnp.min(lo8, axis=0, keepdims=True)
    row_max = jnp.max(hi8, axis=0, keepdims=True)

    def count(pred_fn):
        def body(c, cnt):
            return cnt + _fold8(pred_fn(c, sc_ref[c]).astype(jnp.int32), jnp.add)
        part = lax.fori_loop(0, n_kc, body, jnp.zeros((SUBLANES, tq), jnp.int32))
        return part.sum(axis=0, keepdims=True)

    n_causal = t0 + 1 + lax.broadcasted_iota(jnp.int32, (1, tq), 1)
    short = n_causal <= top_k
    max_ties = count(lambda c, sc: sc >= row_max) >= top_k
    thr0 = jnp.where(short, row_min, row_max)
    done0 = (short | max_ties).astype(jnp.int32)

    def bisect_cond(st):
        it, _, _, _, done = st
        return (it < MAX_BISECT) & (jnp.min(done) == 0)

    def bisect_body(st):
        it, lo, hi, thr, done = st
        mid = 0.5 * lo + 0.5 * hi
        inside = (mid > lo) & (mid < hi)
        cnt = count(lambda c, sc: sc >= mid)
        active = done == 0
        hit = active & inside & (cnt == top_k)
        converged = active & ~inside
        thr = jnp.where(hit, mid, jnp.where(converged, lo, thr))
        done = jnp.where(hit | converged, 1, done)
        lo = jnp.where(active & inside & (cnt > top_k), mid, lo)
        hi = jnp.where(active & inside & (cnt < top_k), mid, hi)
        return it + 1, lo, hi, thr, done

    _, _, _, thr, done = lax.while_loop(bisect_cond, bisect_body,
                                        (jnp.int32(0), row_min, row_max, thr0, done0))

    def to_key(x):
        bits = pltpu.bitcast(x, jnp.int32)
        return bits ^ ((bits >> 31) & 0x7FFFFFFF)

    def exact_kth(_):
        lo = jnp.where(count(lambda c, sc: to_key(sc) >= 0) >= top_k, 0, INT_MIN).astype(jnp.int32)

        def bit_body(b, lo):
            cand = lo + jnp.left_shift(jnp.int32(1), 30 - b)
            return jnp.where(count(lambda c, sc: to_key(sc) >= cand) >= top_k, cand, lo)

        kth = lax.fori_loop(0, 31, bit_body, lo)
        kth = pltpu.bitcast(kth ^ ((kth >> 31) & 0x7FFFFFFF), F32)
        return jnp.where(done == 0, kth, thr)

    thr = lax.cond(jnp.min(done) == 0, exact_kth, lambda _: thr, 0)

    def bias_chunk(c, cnt):
        sel = sc_ref[c] >= thr
        bias_ref[c] = jnp.where(sel, 0.0, MASKED)
        return cnt + _fold8(sel.astype(jnp.int32), jnp.add)

    n_sel = lax.fori_loop(0, n_kc, bias_chunk, jnp.zeros((SUBLANES, tq), jnp.int32))
    n_sel = n_sel.sum(axis=0, keepdims=True)

    @pl.when(jnp.max(n_sel) > top_k)
    def _():
        need = top_k - count(lambda c, sc: sc > thr)

        def cut_body(b, p):
            cand = p + jnp.left_shift(jnp.int32(1), idx_bits - 1 - b)
            below = count(lambda c, sc: (sc == thr) & (c * kc + row < cand))
            return jnp.where(below < need, cand, p)

        cut = lax.fori_loop(0, idx_bits, cut_body, jnp.zeros((1, tq), jnp.int32))

        def rebias_chunk(c, carry):
            sc = sc_ref[c]
            sel = (sc > thr) | ((sc == thr) & (c * kc + row <= cut))
            bias_ref[c] = jnp.where(sel, 0.0, MASKED)
            return carry

        lax.fori_loop(0, n_kc, rebias_chunk, 0)

    m_ref[...] = jnp.full(m_ref.shape, MASKED, F32)
    acc_ref[...] = jnp.zeros(acc_ref.shape, F32)

    def attn_chunk(c, carry):
        bias = bias_ref[c]
        rows = pl.ds(pl.multiple_of(c * kc, kc), kc)
        for n in range(N_KV_HEADS):
            k_c = k_ref[0, rows, n * HEAD_DIM:(n + 1) * HEAD_DIM]
            s_all = jnp.dot(k_c, qg_ref[0, n], preferred_element_type=F32)
            ps, alphas = [], []
            for g in range(GROUP):
                h = n * GROUP + g
                s = s_all[:, g * tq:(g + 1) * tq] + bias
                m_old = m_ref[h]
                m_new = jnp.maximum(m_old, jnp.max(s, axis=0, keepdims=True))
                ps.append(jnp.exp2((s - m_new).astype(BF16)))
                alphas.append(jnp.exp2(m_old - m_new))
                m_ref[h] = m_new
            p = jnp.concatenate(ps, axis=1)
            alpha = jnp.concatenate(alphas, axis=1)
            acc_ref[n] = alpha * acc_ref[n] + jnp.dot(
                vt_ref[0, c, n], p, preferred_element_type=F32)
        return carry

    lax.fori_loop(0, n_kc, attn_chunk, 0)
    for h in range(N_HEADS):
        n, g = divmod(h, GROUP)
        cols = slice(g * tq, (g + 1) * tq)
        o_t = acc_ref[n, :HEAD_DIM, cols] / acc_ref[n, HEAD_DIM:HEAD_DIM + 1, cols]
        o_ref[0, :, h * HEAD_DIM:(h + 1) * HEAD_DIM] = o_t.T.astype(BF16)


def _dsa(qit, wit, ki, qg, k, vt, *, tq, kc):
    b, s, _ = k.shape
    top_k = min(TOPK_MAX, s // 4)
    n_chunks = s // kc
    n_qb = s // tq
    v_rows = vt.shape[3]
    per_batch = pl.Buffered(1)
    return pl.pallas_call(
        functools.partial(_dsa_kernel, tq=tq, kc=kc, top_k=top_k, idx_bits=(s - 1).bit_length()),
        grid=(b, s // tq),
        in_specs=[
            pl.BlockSpec((1, IDX_Q_W, tq), lambda bb, i: (bb, 0, i)),
            pl.BlockSpec((1, IDX_HEADS, tq), lambda bb, i: (bb, 0, i)),
            pl.BlockSpec((1, s, IDX_DIM), lambda bb, i: (bb, 0, 0), pipeline_mode=per_batch),
            pl.BlockSpec((1, N_KV_HEADS, HEAD_DIM, GROUP * tq), lambda bb, i: (bb * n_qb + i, 0, 0, 0)),
            pl.BlockSpec((1, s, KV_W), lambda bb, i: (bb, 0, 0), pipeline_mode=per_batch),
            pl.BlockSpec((1, n_chunks, N_KV_HEADS, v_rows, kc), lambda bb, i: (bb, 0, 0, 0, 0),
                         pipeline_mode=per_batch),
        ],
        out_specs=pl.BlockSpec((1, tq, Q_W), lambda bb, i: (bb, i, 0)),
        out_shape=jax.ShapeDtypeStruct((b, s, Q_W), BF16),
        scratch_shapes=[
            pltpu.VMEM((n_chunks, kc, tq), F32),
            pltpu.VMEM((n_chunks, kc, tq), F32),
            pltpu.VMEM((N_HEADS, 1, tq), F32),
            pltpu.VMEM((N_KV_HEADS, v_rows, GROUP * tq), F32),
        ],
        compiler_params=pltpu.CompilerParams(
            dimension_semantics=("arbitrary", "arbitrary"), vmem_limit_bytes=VMEM_LIMIT),
        name="dsa",
    )(qit, wit, ki, qg, k, vt)


def _out_proj_kernel(a_ref, w_ref, x_ref, o_ref):
    o_ref[...] = x_ref[...] + jnp.dot(a_ref[...], w_ref[...], preferred_element_type=F32)


def _out_proj(a2d, w, x2d, *, layer, tm=512):
    t, k = a2d.shape
    n = w.shape[2]
    return pl.pallas_call(
        _out_proj_kernel,
        grid=(t // tm,),
        in_specs=[
            pl.BlockSpec((tm, k), lambda i: (i, 0)),
            pl.BlockSpec((None, k, n), lambda i: (layer, 0, 0), pipeline_mode=pl.Buffered(1)),
            pl.BlockSpec((tm, n), lambda i: (i, 0)),
        ],
        out_specs=pl.BlockSpec((tm, n), lambda i: (i, 0)),
        out_shape=jax.ShapeDtypeStruct((t, n), F32),
        compiler_params=pltpu.CompilerParams(
            dimension_semantics=("arbitrary",), vmem_limit_bytes=VMEM_LIMIT),
        name="out_proj",
    )(a2d, w, x2d)


def _conv_kernel(x_ref, g_ref, wb_ref, wc_ref, wu_ref, cw_ref, wo_ref, o_ref, h_ref, carry_ref,
                 *, tm, tiles_per_seq):
    i = pl.program_id(0)
    j = pl.program_id(1)

    def mixer_chunk():
        h = h_ref[...]
        b_gate = jnp.dot(h, wb_ref[...], preferred_element_type=F32)
        c_gate = jnp.dot(h, wc_ref[...], preferred_element_type=F32)
        u = jnp.dot(h, wu_ref[...], preferred_element_type=F32)
        z = c_gate * u
        prev = jnp.where(i % tiles_per_seq == 0, 0.0, carry_ref[j])
        carry_ref[j] = z[tm - SUBLANES:, :]
        r = lax.broadcasted_iota(jnp.int32, z.shape, 0)
        z1 = jnp.where(r == 0, prev[7:8, :], pltpu.roll(z, 1, 0))
        z2 = jnp.where(r == 0, prev[6:7, :], jnp.where(r == 1, prev[7:8, :], pltpu.roll(z, 2, 0)))
        y = z2 * cw_ref[0:1, :] + z1 * cw_ref[1:2, :] + z * cw_ref[2:3, :]
        gated = (b_gate * y).astype(BF16)
        return jnp.dot(gated, wo_ref[...], preferred_element_type=F32)

    @pl.when(j == 0)
    def _():
        x = x_ref[...]
        h_ref[...] = _rmsnorm_rows(x, g_ref[...]).astype(BF16)
        o_ref[...] = x + mixer_chunk()

    @pl.when(j > 0)
    def _():
        o_ref[...] += mixer_chunk()


def _conv_mixer(x2d, g, w_in, cw, wo, *, layer, seq, tm=512, tn=512):
    t = x2d.shape[0]
    n_chunks = D_MODEL // tn
    return pl.pallas_call(
        functools.partial(_conv_kernel, tm=tm, tiles_per_seq=seq // tm),
        grid=(t // tm, n_chunks),
        in_specs=[
            pl.BlockSpec((tm, D_MODEL), lambda i, j: (i, 0)),
            pl.BlockSpec((1, D_MODEL), lambda i, j: (0, 0)),
            pl.BlockSpec((None, D_MODEL, tn), lambda i, j: (layer, 0, j)),
            pl.BlockSpec((None, D_MODEL, tn), lambda i, j: (layer, 0, j + n_chunks)),
            pl.BlockSpec((None, D_MODEL, tn), lambda i, j: (layer, 0, j + 2 * n_chunks)),
            pl.BlockSpec((None, 3, tn), lambda i, j: (layer, 0, j)),
            pl.BlockSpec((None, tn, D_MODEL), lambda i, j: (layer, j, 0)),
        ],
        out_specs=pl.BlockSpec((tm, D_MODEL), lambda i, j: (i, 0)),
        out_shape=jax.ShapeDtypeStruct((t, D_MODEL), F32),
        scratch_shapes=[
            pltpu.VMEM((tm, D_MODEL), BF16),
            pltpu.VMEM((n_chunks, SUBLANES, tn), F32),
        ],
        compiler_params=pltpu.CompilerParams(
            dimension_semantics=("arbitrary", "arbitrary"), vmem_limit_bytes=VMEM_LIMIT),
        name="conv_mixer",
    )(x2d, g, w_in, w_in, w_in, cw, wo)


def _mlp_kernel(x_ref, g_ref, w1_ref, w2_ref, o_ref, h_ref):
    j = pl.program_id(1)

    def hidden_chunk():
        a = jnp.maximum(jnp.dot(h_ref[...], w1_ref[...], preferred_element_type=F32), 0.0)
        return jnp.dot((a * a).astype(BF16), w2_ref[...], preferred_element_type=F32)

    @pl.when(j == 0)
    def _():
        x = x_ref[...]
        h_ref[...] = _rmsnorm_rows(x, g_ref[...]).astype(BF16)
        o_ref[...] = x + hidden_chunk()

    @pl.when(j > 0)
    def _():
        o_ref[...] += hidden_chunk()


def _mlp(x2d, g, w1, w2, *, layer, tm=1024, tf=512):
    t = x2d.shape[0]
    return pl.pallas_call(
        _mlp_kernel,
        grid=(t // tm, D_FF // tf),
        in_specs=[
            pl.BlockSpec((tm, D_MODEL), lambda i, j: (i, 0)),
            pl.BlockSpec((1, D_MODEL), lambda i, j: (0, 0)),
            pl.BlockSpec((None, D_MODEL, tf), lambda i, j: (layer, 0, j)),
            pl.BlockSpec((None, tf, D_MODEL), lambda i, j: (layer, j, 0)),
        ],
        out_specs=pl.BlockSpec((tm, D_MODEL), lambda i, j: (i, 0)),
        out_shape=jax.ShapeDtypeStruct((t, D_MODEL), F32),
        scratch_shapes=[pltpu.VMEM((tm, D_MODEL), BF16)],
        compiler_params=pltpu.CompilerParams(
            dimension_semantics=("arbitrary", "arbitrary"), vmem_limit_bytes=VMEM_LIMIT),
        name="mlp",
    )(x2d, g, w1, w2)


def _rope_inv_lanes():
    def inv(rot_dim):
        return ROPE_THETA ** (-jnp.arange(0, rot_dim, 2, dtype=F32) / rot_dim)
    inv_q, inv_i = jnp.tile(inv(ROPE_DIM), 2), jnp.tile(inv(IDX_ROPE_DIM), 2)
    gap = jnp.zeros((IDX_DIM - IDX_ROPE_DIM,), F32)
    tail = jnp.zeros((LANES - IDX_LANE_SHIFT - IDX_DIM - IDX_ROPE_DIM,), F32)
    lanes = jnp.concatenate([inv_q, inv_i, gap, inv_i, tail])
    assert lanes.shape == (LANES,) and IDX_LANE_SHIFT == inv_q.shape[0]
    return lanes[None, :]


def _dsa_layer(x2d, pos2d, batch, seq, g, w_in, w_kw, q_g, k_g, w_out, layer, *, tq=512, kc=512):
    qg, k, vt, qit, ki, wit = _attn_in(
        x2d, pos2d, g[None, :], w_in, w_kw, q_g[None, :], k_g[None, :], _rope_inv_lanes(),
        layer=layer, batch=batch, seq=seq, tq=tq, kc=kc)
    o = _dsa(qit, wit, ki.reshape(batch, seq, IDX_DIM), qg, k.reshape(batch, seq, KV_W),
             vt.reshape((batch, seq // kc) + vt.shape[1:]), tq=tq, kc=kc)
    return _out_proj(o.reshape(batch * seq, Q_W), w_out, x2d, layer=layer)


def kernel(x, positions, attn_norm_g, attn_w_in, attn_q_norm_g, attn_k_norm_g, attn_w_out,
           conv_norm_g, conv_w_in, conv_w, conv_w_out, mlp_norm_g, mlp_w1, mlp_w2):
    batch, seq, d = x.shape
    depth = mlp_w1.shape[0]
    x2d = x.reshape(batch * seq, d)
    pos2d = positions.reshape(batch * seq, 1)
    attn_w_in_b, attn_w_out_b = attn_w_in.astype(BF16), attn_w_out.astype(BF16)
    attn_w_kw_b = jnp.pad(attn_w_in[:, :, MAIN_W:],
                          ((0, 0), (0, 0), (0, LANES - (IDX_DIM + IDX_HEADS)))).astype(BF16)
    conv_w_in_b, conv_w_out_b = conv_w_in.astype(BF16), conv_w_out.astype(BF16)
    mlp_w1_b, mlp_w2_b = mlp_w1.astype(BF16), mlp_w2.astype(BF16)
    for i in range(depth):
        j = i // 2
        if i % 2 == 0:
            x2d = _dsa_layer(x2d, pos2d, batch, seq, attn_norm_g[j], attn_w_in_b, attn_w_kw_b,
                             attn_q_norm_g[j], attn_k_norm_g[j], attn_w_out_b, j)
        else:
            x2d = _conv_mixer(x2d, conv_norm_g[j][None, :], conv_w_in_b, conv_w, conv_w_out_b,
                              layer=j, seq=seq)
        x2d = _mlp(x2d, mlp_norm_g[i][None, :], mlp_w1_b, mlp_w2_b, layer=i)
    return x2d.reshape(batch, seq, d)
```

```python
import functools
import math

import jax
import jax.numpy as jnp
from jax import lax
from jax.experimental import pallas as pl
from jax.experimental.pallas import tpu as pltpu

D_MODEL = 2048
N_HEADS = 16
N_KV_HEADS = 4
HEAD_DIM = 128
GROUP = N_HEADS // N_KV_HEADS
ROPE_DIM = HEAD_DIM // 4
ROPE_THETA = 500000.0
IDX_HEADS = 16
IDX_DIM = 64
IDX_ROPE_DIM = IDX_DIM // 4
TOPK_MAX = 256
Q_W = N_HEADS * HEAD_DIM
KV_W = N_KV_HEADS * HEAD_DIM
IDX_Q_W = IDX_HEADS * IDX_DIM
MAIN_W = Q_W + 2 * KV_W + IDX_Q_W
D_FF = 4 * D_MODEL
EPS = 1e-6
LANES = 128
SUBLANES = 8
VMEM_LIMIT = 56 * 1024 * 1024

INT_MIN = -(2 ** 31)
MASKED = -1e30
MAX_BISECT = 40
Q_SCALE_LOG2E = HEAD_DIM ** -0.5 * math.log2(math.e)

F32 = jnp.float32
BF16 = jnp.bfloat16


def _rmsnorm_rows(x, g):
    ms = jnp.mean(x * x, axis=-1, keepdims=True)
    return (x * lax.rsqrt(ms + EPS)) * g


def _rope_tables(cos, sin, period, half):
    d = lax.broadcasted_iota(jnp.int32, cos.shape, 1) & (period - 1)
    c = jnp.where(d < 2 * half, cos, 1.0)
    s_lo = jnp.where(d < half, -sin, 0.0)
    s_hi = jnp.where((d >= half) & (d < 2 * half), sin, 0.0)
    return c, s_lo, s_hi


def _rope(x, c, s_lo, s_hi, half):
    return x * c + pltpu.roll(x, LANES - half, 1) * s_lo + pltpu.roll(x, half, 1) * s_hi


V_ONES_ROWS = 2 * SUBLANES
IDX_LANE_SHIFT = ROPE_DIM


def _attn_in_kernel(x_ref, pos_ref, g_ref, w_ref, wkw_ref, qgain_ref, kgain_ref, inv_ref,
                    qg_ref, k_ref, vt_ref, qit_ref, ki_ref, wit_ref, h_ref, acc_ref, tq_ref, ti_ref,
                    *, tn, tq, kc):
    j = pl.program_id(1)
    tm = x_ref.shape[0]
    n_q = Q_W // tn
    n_tiles = MAIN_W // tn

    def project(slot):
        acc_ref[slot] = jnp.dot(h_ref[...], w_ref[...], preferred_element_type=F32)

    def prologue():
        h = _rmsnorm_rows(x_ref[...], g_ref[...]).astype(BF16)
        h_ref[...] = h
        ang = pos_ref[...].astype(F32) * inv_ref[...]
        cos, sin = jnp.cos(ang), jnp.sin(ang)
        cq, sq_lo, sq_hi = _rope_tables(cos, sin, HEAD_DIM, ROPE_DIM // 2)
        tq_ref[0], tq_ref[1], tq_ref[2] = cq, sq_lo, sq_hi
        back = LANES - IDX_LANE_SHIFT
        ci, si_lo, si_hi = _rope_tables(pltpu.roll(cos, back, 1), pltpu.roll(sin, back, 1),
                                        IDX_DIM, IDX_ROPE_DIM // 2)
        ti_ref[0], ti_ref[1], ti_ref[2] = ci, si_lo, si_hi
        kw = jnp.dot(h, wkw_ref[...], preferred_element_type=F32)
        roped = _rope(kw, ci, si_lo, si_hi, IDX_ROPE_DIM // 2)
        ki_ref[...] = roped[:, :IDX_DIM].astype(BF16)
        scale = IDX_HEADS ** -0.5 * IDX_DIM ** -0.5
        wit_ref[0] = (kw * scale).T[IDX_DIM:IDX_DIM + IDX_HEADS, :]

    def head_norm_rope(acc, hh, gain_ref):
        y = _rmsnorm_rows(acc[:, hh * HEAD_DIM:(hh + 1) * HEAD_DIM], gain_ref[...])
        return _rope(y, tq_ref[0], tq_ref[1], tq_ref[2], ROPE_DIM // 2)

    def finish(tile, slot):
        acc = acc_ref[slot]
        if tile < n_q:
            for g in range(GROUP):
                y = head_norm_rope(acc, g, qgain_ref) * Q_SCALE_LOG2E
                for r in range(tm // tq):
                    qg_ref[r, 0, :, g * tq:(g + 1) * tq] = y[r * tq:(r + 1) * tq, :].T.astype(BF16)
        elif tile == n_q:
            for hh in range(N_KV_HEADS):
                k_ref[:, hh * HEAD_DIM:(hh + 1) * HEAD_DIM] = (
                    head_norm_rope(acc, hh, kgain_ref).astype(BF16))
        elif tile == n_q + 1:
            for n in range(N_KV_HEADS):
                for cc in range(tm // kc):
                    v = acc[cc * kc:(cc + 1) * kc, n * HEAD_DIM:(n + 1) * HEAD_DIM]
                    vt_ref[cc, n, :HEAD_DIM, :] = v.T.astype(BF16)
                    vt_ref[cc, n, HEAD_DIM:, :] = jnp.ones((V_ONES_ROWS, kc), BF16)
        else:
            for hh in range(tn // LANES):
                sl = slice(hh * LANES, (hh + 1) * LANES)
                y = _rope(acc[:, sl], ti_ref[0], ti_ref[1], ti_ref[2], IDX_ROPE_DIM // 2)
                qit_ref[0, sl, :] = y.T.astype(BF16)

    for step in range(n_tiles + 1):
        @pl.when(j == step)
        def _(step=step):
            if step == 0:
                prologue()
            if step < n_tiles:
                project(step % 2)
            if step > 0:
                finish(step - 1, (step - 1) % 2)


def _attn_in(x2d, pos2d, g, w_in, w_kw, q_g, k_g, inv, *, layer, batch, seq, tq, kc, tm=1024, tn=512):
    t = x2d.shape[0]
    assert tn == KV_W == GROUP * HEAD_DIM and tm % tq == 0 and tm % kc == 0 and seq % tm == 0
    assert w_in.shape[2:] == (D_MODEL, tn)
    tiles_per_seq = seq // tm
    n_q = Q_W // tn
    n_tiles = MAIN_W // tn
    first_qi = n_tiles - IDX_Q_W // tn
    return pl.pallas_call(
        functools.partial(_attn_in_kernel, tn=tn, tq=tq, kc=kc),
        grid=(t // tm, n_tiles + 1),
        in_specs=[
            pl.BlockSpec((tm, D_MODEL), lambda i, j: (i, 0)),
            pl.BlockSpec((tm, 1), lambda i, j: (i, 0)),
            pl.BlockSpec((1, D_MODEL), lambda i, j: (0, 0)),
            pl.BlockSpec((None, None, D_MODEL, tn),
                         lambda i, j: (layer, jnp.minimum(j, n_tiles - 1), 0, 0)),
            pl.BlockSpec((None, D_MODEL, LANES), lambda i, j: (layer, 0, 0)),
            pl.BlockSpec((1, HEAD_DIM), lambda i, j: (0, 0)),
            pl.BlockSpec((1, HEAD_DIM), lambda i, j: (0, 0)),
            pl.BlockSpec((1, LANES), lambda i, j: (0, 0)),
        ],
        out_specs=[
            pl.BlockSpec((tm // tq, 1, HEAD_DIM, GROUP * tq),
                         lambda i, j: (i, jnp.clip(j - 1, 0, n_q - 1), 0, 0)),
            pl.BlockSpec((tm, KV_W), lambda i, j: (i, 0)),
            pl.BlockSpec((tm // kc, N_KV_HEADS, HEAD_DIM + V_ONES_ROWS, kc), lambda i, j: (i, 0, 0, 0)),
            pl.BlockSpec((1, tn, tm),
                         lambda i, j: (i // tiles_per_seq, jnp.maximum(j - 1 - first_qi, 0),
                                       i % tiles_per_seq)),
            pl.BlockSpec((tm, IDX_DIM), lambda i, j: (i, 0)),
            pl.BlockSpec((1, IDX_HEADS, tm), lambda i, j: (i // tiles_per_seq, 0, i % tiles_per_seq)),
        ],
        out_shape=[
            jax.ShapeDtypeStruct((t // tq, N_KV_HEADS, HEAD_DIM, GROUP * tq), BF16),
            jax.ShapeDtypeStruct((t, KV_W), BF16),
            jax.ShapeDtypeStruct((t // kc, N_KV_HEADS, HEAD_DIM + V_ONES_ROWS, kc), BF16),
            jax.ShapeDtypeStruct((batch, IDX_Q_W, seq), BF16),
            jax.ShapeDtypeStruct((t, IDX_DIM), BF16),
            jax.ShapeDtypeStruct((batch, IDX_HEADS, seq), F32),
        ],
        scratch_shapes=[
            pltpu.VMEM((tm, D_MODEL), BF16),
            pltpu.VMEM((2, tm, tn), F32),
            pltpu.VMEM((3, tm, LANES), F32),
            pltpu.VMEM((3, tm, LANES), F32),
        ],
        compiler_params=pltpu.CompilerParams(
            dimension_semantics=("arbitrary", "arbitrary"), vmem_limit_bytes=VMEM_LIMIT),
        name="attn_in",
    )(x2d, pos2d, g, w_in, w_kw, q_g, k_g, inv)


def _fold8(x, op):
    rows, cols = x.shape
    x = x.reshape(rows // SUBLANES, SUBLANES, cols)
    if op is jnp.add:
        return x.sum(axis=0)
    return x.min(axis=0) if op is jnp.minimum else x.max(axis=0)


def _dsa_kernel(qit_ref, wit_ref, ki_ref, qg_ref, k_ref, vt_ref, o_ref, sc_ref, bias_ref,
                m_ref, acc_ref, *, tq, kc, top_k, idx_bits):
    i = pl.program_id(1)
    t0 = i * tq
    n_kc = (t0 + tq + kc - 1) // kc
    row = lax.broadcasted_iota(jnp.int32, (kc, tq), 0)
    t_idx = t0 + lax.broadcasted_iota(jnp.int32, (kc, tq), 1)

    def score_chunk(c, carry):
        lo8, hi8 = carry
        k_rows = ki_ref[0, pl.ds(pl.multiple_of(c * kc, kc), kc), :]
        acc = jnp.zeros((kc, tq), F32)
        for h in range(IDX_HEADS):
            rel = jnp.dot(k_rows, qit_ref[0, h * IDX_DIM:(h + 1) * IDX_DIM, :],
                          preferred_element_type=F32)
            acc = acc + wit_ref[0, h:h + 1, :] * jnp.maximum(rel, 0.0)
        causal = c * kc + row <= t_idx
        sc_ref[c] = jnp.where(causal, acc, -jnp.inf)
        lo8 = jnp.minimum(lo8, _fold8(jnp.where(causal, acc, jnp.inf), jnp.minimum))
        hi8 = jnp.maximum(hi8, _fold8(jnp.where(causal, acc, -jnp.inf), jnp.maximum))
        return lo8, hi8

    lo8, hi8 = lax.fori_loop(0, n_kc, score_chunk, (jnp.full((SUBLANES, tq), jnp.inf, F32),
                                                     jnp.full((SUBLANES, tq), -jnp.inf, F32)))
    row_min = jnp.min(lo8, axis=0, keepdims=True)
    row_max = jnp.max(hi8, axis=0, keepdims=True)

    def count(pred_fn):
        def body(c, cnt):
            return cnt + _fold8(pred_fn(c, sc_ref[c]).astype(jnp.int32), jnp.add)
        part = lax.fori_loop(0, n_kc, body, jnp.zeros((SUBLANES, tq), jnp.int32))
        return part.sum(axis=0, keepdims=True)

    n_causal = t0 + 1 + lax.broadcasted_iota(jnp.int32, (1, tq), 1)
    short = n_causal <= top_k
    max_ties = count(lambda c, sc: sc >= row_max) >= top_k
    thr0 = jnp.where(short, row_min, row_max)
    done0 = (short | max_ties).astype(jnp.int32)

    def bisect_cond(st):
        it, _, _, _, done = st
        return (it < MAX_BISECT) & (jnp.min(done) == 0)

    def bisect_body(st):
        it, lo, hi, thr, done = st
        mid = 0.5 * lo + 0.5 * hi
        inside = (mid > lo) & (mid < hi)
        cnt = count(lambda c, sc: sc >= mid)
        active = done == 0
        hit = active & inside & (cnt == top_k)
        converged = active & ~inside
        thr = jnp.where(hit, mid, jnp.where(converged, lo, thr))
        done = jnp.where(hit | converged, 1, done)
        lo = jnp.where(active & inside & (cnt > top_k), mid, lo)
        hi = jnp.where(active & inside & (cnt < top_k), mid, hi)
        return it + 1, lo, hi, thr, done

    _, _, _, thr, done = lax.while_loop(bisect_cond, bisect_body,
                                        (jnp.int32(0), row_min, row_max, thr0, done0))

    def to_key(x):
        bits = pltpu.bitcast(x, jnp.int32)
        return bits ^ ((bits >> 31) & 0x7FFFFFFF)

    def exact_kth(_):
        lo = jnp.where(count(lambda c, sc: to_key(sc) >= 0) >= top_k, 0, INT_MIN).astype(jnp.int32)

        def bit_body(b, lo):
            cand = lo + jnp.left_shift(jnp.int32(1), 30 - b)
            return jnp.where(count(lambda c, sc: to_key(sc) >= cand) >= top_k, cand, lo)

        kth = lax.fori_loop(0, 31, bit_body, lo)
        kth = pltpu.bitcast(kth ^ ((kth >> 31) & 0x7FFFFFFF), F32)
        return jnp.where(done == 0, kth, thr)

    thr = lax.cond(jnp.min(done) == 0, exact_kth, lambda _: thr, 0)

    def bias_chunk(c, cnt):
        sel = sc_ref[c] >= thr
        bias_ref[c] = jnp.where(sel, 0.0, MASKED)
        return cnt + _fold8(sel.astype(jnp.int32), jnp.add)

    n_sel = lax.fori_loop(0, n_kc, bias_chunk, jnp.zeros((SUBLANES, tq), jnp.int32))
    n_sel = n_sel.sum(axis=0, keepdims=True)

    @pl.when(jnp.max(n_sel) > top_k)
    def _():
        need = top_k - count(lambda c, sc: sc > thr)

        def cut_body(b, p):
            cand = p + jnp.left_shift(jnp.int32(1), idx_bits - 1 - b)
            below = count(lambda c, sc: (sc == thr) & (c * kc + row < cand))
            return jnp.where(below < need, cand, p)

        cut = lax.fori_loop(0, idx_bits, cut_body, jnp.zeros((1, tq), jnp.int32))

        def rebias_chunk(c, carry):
            sc = sc_ref[c]
            sel = (sc > thr) | ((sc == thr) & (c * kc + row <= cut))
            bias_ref[c] = jnp.where(sel, 0.0, MASKED)
            return carry

        lax.fori_loop(0, n_kc, rebias_chunk, 0)

    m_ref[...] = jnp.full(m_ref.shape, MASKED, F32)
    acc_ref[...] = jnp.zeros(acc_ref.shape, F32)

    def attn_chunk(c, carry):
        bias = bias_ref[c]
        rows = pl.ds(pl.multiple_of(c * kc, kc), kc)
        for n in range(N_KV_HEADS):
            k_c = k_ref[0, rows, n * HEAD_DIM:(n + 1) * HEAD_DIM]
            s_all = jnp.dot(k_c, qg_ref[0, n], preferred_element_type=F32)
            ps, alphas = [], []
            for g in range(GROUP):
                h = n * GROUP + g
                s = s_all[:, g * tq:(g + 1) * tq] + bias
                m_old = m_ref[h]
                m_new = jnp.maximum(m_old, jnp.max(s, axis=0, keepdims=True))
                ps.append(jnp.exp2((s - m_new).astype(BF16)))
                alphas.append(jnp.exp2(m_old - m_new))
                m_ref[h] = m_new
            p = jnp.concatenate(ps, axis=1)
            alpha = jnp.concatenate(alphas, axis=1)
            acc_ref[n] = alpha * acc_ref[n] + jnp.dot(
                vt_ref[0, c, n], p, preferred_element_type=F32)
        return carry

    lax.fori_loop(0, n_kc, attn_chunk, 0)
    for h in range(N_HEADS):
        n, g = divmod(h, GROUP)
        cols = slice(g * tq, (g + 1) * tq)
        o_t = acc_ref[n, :HEAD_DIM, cols] / acc_ref[n, HEAD_DIM:HEAD_DIM + 1, cols]
        o_ref[0, :, h * HEAD_DIM:(h + 1) * HEAD_DIM] = o_t.T.astype(BF16)


def _dsa(qit, wit, ki, qg, k, vt, *, tq, kc):
    b, s, _ = k.shape
    top_k = min(TOPK_MAX, s // 4)
    n_chunks = s // kc
    n_qb = s // tq
    v_rows = vt.shape[3]
    per_batch = pl.Buffered(1)
    return pl.pallas_call(
        functools.partial(_dsa_kernel, tq=tq, kc=kc, top_k=top_k, idx_bits=(s - 1).bit_length()),
        grid=(b, s // tq),
        in_specs=[
            pl.BlockSpec((1, IDX_Q_W, tq), lambda bb, i: (bb, 0, i)),
            pl.BlockSpec((1, IDX_HEADS, tq), lambda bb, i: (bb, 0, i)),
            pl.BlockSpec((1, s, IDX_DIM), lambda bb, i: (bb, 0, 0), pipeline_mode=per_batch),
            pl.BlockSpec((1, N_KV_HEADS, HEAD_DIM, GROUP * tq), lambda bb, i: (bb * n_qb + i, 0, 0, 0)),
            pl.BlockSpec((1, s, KV_W), lambda bb, i: (bb, 0, 0), pipeline_mode=per_batch),
            pl.BlockSpec((1, n_chunks, N_KV_HEADS, v_rows, kc), lambda bb, i: (bb, 0, 0, 0, 0),
                         pipeline_mode=per_batch),
        ],
        out_specs=pl.BlockSpec((1, tq, Q_W), lambda bb, i: (bb, i, 0)),
        out_shape=jax.ShapeDtypeStruct((b, s, Q_W), BF16),
        scratch_shapes=[
            pltpu.VMEM((n_chunks, kc, tq), F32),
            pltpu.VMEM((n_chunks, kc, tq), F32),
            pltpu.VMEM((N_HEADS, 1, tq), F32),
            pltpu.VMEM((N_KV_HEADS, v_rows, GROUP * tq), F32),
        ],
        compiler_params=pltpu.CompilerParams(
            dimension_semantics=("arbitrary", "arbitrary"), vmem_limit_bytes=VMEM_LIMIT),
        name="dsa",
    )(qit, wit, ki, qg, k, vt)


def _out_proj_kernel(a_ref, w_ref, x_ref, o_ref):
    o_ref[...] = x_ref[...] + jnp.dot(a_ref[...], w_ref[...], preferred_element_type=F32)


def _out_proj(a2d, w, x2d, *, layer, tm=512):
    t, k = a2d.shape
    n = w.shape[2]
    return pl.pallas_call(
        _out_proj_kernel,
        grid=(t // tm,),
        in_specs=[
            pl.BlockSpec((tm, k), lambda i: (i, 0)),
            pl.BlockSpec((None, k, n), lambda i: (layer, 0, 0), pipeline_mode=pl.Buffered(1)),
            pl.BlockSpec((tm, n), lambda i: (i, 0)),
        ],
        out_specs=pl.BlockSpec((tm, n), lambda i: (i, 0)),
        out_shape=jax.ShapeDtypeStruct((t, n), F32),
        compiler_params=pltpu.CompilerParams(
            dimension_semantics=("arbitrary",), vmem_limit_bytes=VMEM_LIMIT),
        name="out_proj",
    )(a2d, w, x2d)


def _conv_kernel(x_ref, g_ref, wb_ref, wc_ref, wu_ref, cw_ref, wo_ref, o_ref, h_ref, bz_ref, carry_ref,
                 *, tm, tiles_per_seq, n_chunks):
    i = pl.program_id(0)
    j = pl.program_id(1)

    def project(slot):
        h = h_ref[...]
        bz_ref[slot, 0] = jnp.dot(h, wb_ref[...], preferred_element_type=F32)
        bz_ref[slot, 1] = (jnp.dot(h, wc_ref[...], preferred_element_type=F32)
                           * jnp.dot(h, wu_ref[...], preferred_element_type=F32))

    def finish(chunk, slot):
        b_gate, z = bz_ref[slot, 0], bz_ref[slot, 1]
        prev = jnp.where(i % tiles_per_seq == 0, 0.0, carry_ref[chunk])
        carry_ref[chunk] = z[tm - SUBLANES:, :]
        r = lax.broadcasted_iota(jnp.int32, z.shape, 0)
        z1 = jnp.where(r == 0, prev[7:8, :], pltpu.roll(z, 1, 0))
        z2 = jnp.where(r == 0, prev[6:7, :], jnp.where(r == 1, prev[7:8, :], pltpu.roll(z, 2, 0)))
        y = z2 * cw_ref[0:1, :] + z1 * cw_ref[1:2, :] + z * cw_ref[2:3, :]
        gated = (b_gate * y).astype(BF16)
        out = jnp.dot(gated, wo_ref[...], preferred_element_type=F32)
        if chunk == 0:
            o_ref[...] = x_ref[...] + out
        else:
            o_ref[...] += out

    for step in range(n_chunks + 1):
        @pl.when(j == step)
        def _(step=step):
            if step == 0:
                h_ref[...] = _rmsnorm_rows(x_ref[...], g_ref[...]).astype(BF16)
            if step < n_chunks:
                project(step % 2)
            if step > 0:
                finish(step - 1, (step - 1) % 2)


def _conv_mixer(x2d, g, w_in, cw, wo, *, layer, seq, tm=512, tn=512):
    t = x2d.shape[0]
    n_chunks = D_MODEL // tn
    assert w_in.shape[1:] == (3 * n_chunks, D_MODEL, tn)
    gate_tile = lambda j: jnp.minimum(j, n_chunks - 1)
    done_tile = lambda j: jnp.maximum(j - 1, 0)
    return pl.pallas_call(
        functools.partial(_conv_kernel, tm=tm, tiles_per_seq=seq // tm, n_chunks=n_chunks),
        grid=(t // tm, n_chunks + 1),
        in_specs=[
            pl.BlockSpec((tm, D_MODEL), lambda i, j: (i, 0)),
            pl.BlockSpec((1, D_MODEL), lambda i, j: (0, 0)),
            pl.BlockSpec((None, None, D_MODEL, tn), lambda i, j: (layer, gate_tile(j), 0, 0)),
            pl.BlockSpec((None, None, D_MODEL, tn), lambda i, j: (layer, gate_tile(j) + n_chunks, 0, 0)),
            pl.BlockSpec((None, None, D_MODEL, tn),
                         lambda i, j: (layer, gate_tile(j) + 2 * n_chunks, 0, 0)),
            pl.BlockSpec((None, 3, tn), lambda i, j: (layer, 0, done_tile(j))),
            pl.BlockSpec((None, tn, D_MODEL), lambda i, j: (layer, done_tile(j), 0)),
        ],
        out_specs=pl.BlockSpec((tm, D_MODEL), lambda i, j: (i, 0)),
        out_shape=jax.ShapeDtypeStruct((t, D_MODEL), F32),
        scratch_shapes=[
            pltpu.VMEM((tm, D_MODEL), BF16),
            pltpu.VMEM((2, 2, tm, tn), F32),
            pltpu.VMEM((n_chunks, SUBLANES, tn), F32),
        ],
        compiler_params=pltpu.CompilerParams(
            dimension_semantics=("arbitrary", "arbitrary"), vmem_limit_bytes=VMEM_LIMIT),
        name="conv_mixer",
    )(x2d, g, w_in, w_in, w_in, cw, wo)


def _mlp_kernel(x_ref, g_ref, w1_ref, w2_ref, o_ref, h_ref):
    j = pl.program_id(1)

    def hidden_chunk():
        a = jnp.maximum(jnp.dot(h_ref[...], w1_ref[...], preferred_element_type=F32), 0.0)
        return jnp.dot((a * a).astype(BF16), w2_ref[...], preferred_element_type=F32)

    @pl.when(j == 0)
    def _():
        x = x_ref[...]
        h_ref[...] = _rmsnorm_rows(x, g_ref[...]).astype(BF16)
        o_ref[...] = x + hidden_chunk()

    @pl.when(j > 0)
    def _():
        o_ref[...] += hidden_chunk()


def _mlp(x2d, g, w1, w2, *, layer, tm=1024, tf=512):
    t = x2d.shape[0]
    assert w1.shape[1:] == (D_FF // tf, D_MODEL, tf)
    return pl.pallas_call(
        _mlp_kernel,
        grid=(t // tm, D_FF // tf),
        in_specs=[
            pl.BlockSpec((tm, D_MODEL), lambda i, j: (i, 0)),
            pl.BlockSpec((1, D_MODEL), lambda i, j: (0, 0)),
            pl.BlockSpec((None, None, D_MODEL, tf), lambda i, j: (layer, j, 0, 0)),
            pl.BlockSpec((None, tf, D_MODEL), lambda i, j: (layer, j, 0)),
        ],
        out_specs=pl.BlockSpec((tm, D_MODEL), lambda i, j: (i, 0)),
        out_shape=jax.ShapeDtypeStruct((t, D_MODEL), F32),
        scratch_shapes=[pltpu.VMEM((tm, D_MODEL), BF16)],
        compiler_params=pltpu.CompilerParams(
            dimension_semantics=("arbitrary", "arbitrary"), vmem_limit_bytes=VMEM_LIMIT),
        name="mlp",
    )(x2d, g, w1, w2)


W_TILE = 512


def _column_tiles(w, tile):
    layers, rows, cols = w.shape
    return w.reshape(layers, rows, cols // tile, tile).transpose(0, 2, 1, 3)


def _rope_inv_lanes():
    def inv(rot_dim):
        return ROPE_THETA ** (-jnp.arange(0, rot_dim, 2, dtype=F32) / rot_dim)
    inv_q, inv_i = jnp.tile(inv(ROPE_DIM), 2), jnp.tile(inv(IDX_ROPE_DIM), 2)
    gap = jnp.zeros((IDX_DIM - IDX_ROPE_DIM,), F32)
    tail = jnp.zeros((LANES - IDX_LANE_SHIFT - IDX_DIM - IDX_ROPE_DIM,), F32)
    lanes = jnp.concatenate([inv_q, inv_i, gap, inv_i, tail])
    assert lanes.shape == (LANES,) and IDX_LANE_SHIFT == inv_q.shape[0]
    return lanes[None, :]


def _dsa_layer(x2d, pos2d, batch, seq, g, w_in, w_kw, q_g, k_g, w_out, layer, *, tq=512, kc=512):
    qg, k, vt, qit, ki, wit = _attn_in(
        x2d, pos2d, g[None, :], w_in, w_kw, q_g[None, :], k_g[None, :], _rope_inv_lanes(),
        layer=layer, batch=batch, seq=seq, tq=tq, kc=kc)
    o = _dsa(qit, wit, ki.reshape(batch, seq, IDX_DIM), qg, k.reshape(batch, seq, KV_W),
             vt.reshape((batch, seq // kc) + vt.shape[1:]), tq=tq, kc=kc)
    return _out_proj(o.reshape(batch * seq, Q_W), w_out, x2d, layer=layer)


def kernel(x, positions, attn_norm_g, attn_w_in, attn_q_norm_g, attn_k_norm_g, attn_w_out,
           conv_norm_g, conv_w_in, conv_w, conv_w_out, mlp_norm_g, mlp_w1, mlp_w2):
    batch, seq, d = x.shape
    depth = mlp_w1.shape[0]
    x2d = x.reshape(batch * seq, d)
    pos2d = positions.reshape(batch * seq, 1)
    attn_w_in_b = _column_tiles(attn_w_in[:, :, :MAIN_W].astype(BF16), W_TILE)
    attn_w_kw_b = jnp.pad(attn_w_in[:, :, MAIN_W:],
                          ((0, 0), (0, 0), (0, LANES - (IDX_DIM + IDX_HEADS)))).astype(BF16)
    attn_w_out_b = attn_w_out.astype(BF16)
    conv_w_in_b, conv_w_out_b = _column_tiles(conv_w_in.astype(BF16), W_TILE), conv_w_out.astype(BF16)
    mlp_w1_b, mlp_w2_b = _column_tiles(mlp_w1.astype(BF16), W_TILE), mlp_w2.astype(BF16)
    for i in range(depth):
        j = i // 2
        if i % 2 == 0:
            x2d = _dsa_layer(x2d, pos2d, batch, seq, attn_norm_g[j], attn_w_in_b, attn_w_kw_b,
                             attn_q_norm_g[j], attn_k_norm_g[j], attn_w_out_b, j)
        else:
            x2d = _conv_mixer(x2d, conv_norm_g[j][None, :], conv_w_in_b, conv_w, conv_w_out_b,
                              layer=j, seq=seq)
        x2d = _mlp(x2d, mlp_norm_g[i][None, :], mlp_w1_b, mlp_w2_b, layer=i)
    return x2d.reshape(batch, seq, d)
```

```python
import functools
import math

import jax
import jax.numpy as jnp
from jax import lax
from jax.experimental import pallas as pl
from jax.experimental.pallas import tpu as pltpu

D_MODEL = 2048
N_HEADS = 16
N_KV_HEADS = 4
HEAD_DIM = 128
GROUP = N_HEADS // N_KV_HEADS
ROPE_DIM = HEAD_DIM // 4
ROPE_THETA = 500000.0
IDX_HEADS = 16
IDX_DIM = 64
IDX_ROPE_DIM = IDX_DIM // 4
TOPK_MAX = 256
Q_W = N_HEADS * HEAD_DIM
KV_W = N_KV_HEADS * HEAD_DIM
IDX_Q_W = IDX_HEADS * IDX_DIM
MAIN_W = Q_W + 2 * KV_W + IDX_Q_W
D_FF = 4 * D_MODEL
EPS = 1e-6
LANES = 128
SUBLANES = 8
VMEM_LIMIT = 56 * 1024 * 1024

INT_MIN = -(2 ** 31)
MASKED = -1e30
MAX_BISECT = 40
BISECT_UNROLL = 2
Q_SCALE_LOG2E = HEAD_DIM ** -0.5 * math.log2(math.e)

F32 = jnp.float32
BF16 = jnp.bfloat16


def _rmsnorm_rows(x, g):
    ms = jnp.mean(x * x, axis=-1, keepdims=True)
    return (x * lax.rsqrt(ms + EPS)) * g


def _rope_tables(cos, sin, period, half):
    d = lax.broadcasted_iota(jnp.int32, cos.shape, 1) & (period - 1)
    c = jnp.where(d < 2 * half, cos, 1.0)
    s_lo = jnp.where(d < half, -sin, 0.0)
    s_hi = jnp.where((d >= half) & (d < 2 * half), sin, 0.0)
    return c, s_lo, s_hi


def _rope(x, c, s_lo, s_hi, half):
    return x * c + pltpu.roll(x, LANES - half, 1) * s_lo + pltpu.roll(x, half, 1) * s_hi


V_ONES_ROWS = 2 * SUBLANES
IDX_LANE_SHIFT = ROPE_DIM


def _attn_in_kernel(x_ref, pos_ref, g_ref, w_ref, wkw_ref, qgain_ref, kgain_ref, inv_ref,
                    qg_ref, k_ref, vt_ref, qit_ref, ki_ref, wit_ref, h_ref, acc_ref, tq_ref, ti_ref,
                    *, tn, tq, kc):
    j = pl.program_id(1)
    tm = x_ref.shape[0]
    n_q = Q_W // tn
    n_tiles = MAIN_W // tn

    def project(slot):
        acc_ref[slot] = jnp.dot(h_ref[...], w_ref[...], preferred_element_type=F32)

    def prologue():
        h = _rmsnorm_rows(x_ref[...], g_ref[...]).astype(BF16)
        h_ref[...] = h
        ang = pos_ref[...].astype(F32) * inv_ref[...]
        cos, sin = jnp.cos(ang), jnp.sin(ang)
        cq, sq_lo, sq_hi = _rope_tables(cos, sin, HEAD_DIM, ROPE_DIM // 2)
        tq_ref[0], tq_ref[1], tq_ref[2] = cq, sq_lo, sq_hi
        back = LANES - IDX_LANE_SHIFT
        ci, si_lo, si_hi = _rope_tables(pltpu.roll(cos, back, 1), pltpu.roll(sin, back, 1),
                                        IDX_DIM, IDX_ROPE_DIM // 2)
        ti_ref[0], ti_ref[1], ti_ref[2] = ci, si_lo, si_hi
        kw = jnp.dot(h, wkw_ref[...], preferred_element_type=F32)
        roped = _rope(kw, ci, si_lo, si_hi, IDX_ROPE_DIM // 2)
        ki_ref[...] = roped[:, :IDX_DIM].astype(BF16)
        scale = IDX_HEADS ** -0.5 * IDX_DIM ** -0.5
        wit_ref[0] = (kw * scale).T[IDX_DIM:IDX_DIM + IDX_HEADS, :]

    def head_norm_rope(acc, hh, gain_ref):
        y = _rmsnorm_rows(acc[:, hh * HEAD_DIM:(hh + 1) * HEAD_DIM], gain_ref[...])
        return _rope(y, tq_ref[0], tq_ref[1], tq_ref[2], ROPE_DIM // 2)

    def finish(tile, slot):
        acc = acc_ref[slot]
        if tile < n_q:
            for g in range(GROUP):
                y = head_norm_rope(acc, g, qgain_ref) * Q_SCALE_LOG2E
                for r in range(tm // tq):
                    qg_ref[r, 0, :, g * tq:(g + 1) * tq] = y[r * tq:(r + 1) * tq, :].T.astype(BF16)
        elif tile == n_q:
            for hh in range(N_KV_HEADS):
                k_ref[:, hh * HEAD_DIM:(hh + 1) * HEAD_DIM] = (
                    head_norm_rope(acc, hh, kgain_ref).astype(BF16))
        elif tile == n_q + 1:
            for n in range(N_KV_HEADS):
                for cc in range(tm // kc):
                    v = acc[cc * kc:(cc + 1) * kc, n * HEAD_DIM:(n + 1) * HEAD_DIM]
                    vt_ref[cc, n, :HEAD_DIM, :] = v.T.astype(BF16)
                    vt_ref[cc, n, HEAD_DIM:, :] = jnp.ones((V_ONES_ROWS, kc), BF16)
        else:
            for hh in range(tn // LANES):
                sl = slice(hh * LANES, (hh + 1) * LANES)
                y = _rope(acc[:, sl], ti_ref[0], ti_ref[1], ti_ref[2], IDX_ROPE_DIM // 2)
                qit_ref[0, sl, :] = y.T.astype(BF16)

    for step in range(n_tiles + 1):
        @pl.when(j == step)
        def _(step=step):
            if step == 0:
                prologue()
            if step < n_tiles:
                project(step % 2)
            if step > 0:
                finish(step - 1, (step - 1) % 2)


def _attn_in(x2d, pos2d, g, w_in, w_kw, q_g, k_g, inv, *, layer, batch, seq, tq, kc, tm=1024, tn=512):
    t = x2d.shape[0]
    assert tn == KV_W == GROUP * HEAD_DIM and tm % tq == 0 and tm % kc == 0 and seq % tm == 0
    tiles_per_seq = seq // tm
    n_q = Q_W // tn
    n_tiles = MAIN_W // tn
    first_qi = n_tiles - IDX_Q_W // tn
    return pl.pallas_call(
        functools.partial(_attn_in_kernel, tn=tn, tq=tq, kc=kc),
        grid=(t // tm, n_tiles + 1),
        in_specs=[
            pl.BlockSpec((tm, D_MODEL), lambda i, j: (i, 0)),
            pl.BlockSpec((tm, 1), lambda i, j: (i, 0)),
            pl.BlockSpec((1, D_MODEL), lambda i, j: (0, 0)),
            pl.BlockSpec((None, D_MODEL, tn), lambda i, j: (layer, 0, jnp.minimum(j, n_tiles - 1))),
            pl.BlockSpec((None, D_MODEL, LANES), lambda i, j: (layer, 0, 0)),
            pl.BlockSpec((1, HEAD_DIM), lambda i, j: (0, 0)),
            pl.BlockSpec((1, HEAD_DIM), lambda i, j: (0, 0)),
            pl.BlockSpec((1, LANES), lambda i, j: (0, 0)),
        ],
        out_specs=[
            pl.BlockSpec((tm // tq, 1, HEAD_DIM, GROUP * tq),
                         lambda i, j: (i, jnp.clip(j - 1, 0, n_q - 1), 0, 0)),
            pl.BlockSpec((tm, KV_W), lambda i, j: (i, 0)),
            pl.BlockSpec((tm // kc, N_KV_HEADS, HEAD_DIM + V_ONES_ROWS, kc), lambda i, j: (i, 0, 0, 0)),
            pl.BlockSpec((1, tn, tm),
                         lambda i, j: (i // tiles_per_seq, jnp.maximum(j - 1 - first_qi, 0),
                                       i % tiles_per_seq)),
            pl.BlockSpec((tm, IDX_DIM), lambda i, j: (i, 0)),
            pl.BlockSpec((1, IDX_HEADS, tm), lambda i, j: (i // tiles_per_seq, 0, i % tiles_per_seq)),
        ],
        out_shape=[
            jax.ShapeDtypeStruct((t // tq, N_KV_HEADS, HEAD_DIM, GROUP * tq), BF16),
            jax.ShapeDtypeStruct((t, KV_W), BF16),
            jax.ShapeDtypeStruct((t // kc, N_KV_HEADS, HEAD_DIM + V_ONES_ROWS, kc), BF16),
            jax.ShapeDtypeStruct((batch, IDX_Q_W, seq), BF16),
            jax.ShapeDtypeStruct((t, IDX_DIM), BF16),
            jax.ShapeDtypeStruct((batch, IDX_HEADS, seq), F32),
        ],
        scratch_shapes=[
            pltpu.VMEM((tm, D_MODEL), BF16),
            pltpu.VMEM((2, tm, tn), F32),
            pltpu.VMEM((3, tm, LANES), F32),
            pltpu.VMEM((3, tm, LANES), F32),
        ],
        compiler_params=pltpu.CompilerParams(
            dimension_semantics=("arbitrary", "arbitrary"), vmem_limit_bytes=VMEM_LIMIT),
        name="attn_in",
    )(x2d, pos2d, g, w_in, w_kw, q_g, k_g, inv)


def _fold8(x, op):
    rows, cols = x.shape
    x = x.reshape(rows // SUBLANES, SUBLANES, cols)
    if op is jnp.add:
        return x.sum(axis=0)
    return x.min(axis=0) if op is jnp.minimum else x.max(axis=0)


def _dsa_kernel(qit_ref, wit_ref, ki_ref, qg_ref, k_ref, vt_ref, o_ref, sc_ref, bias_ref,
                m_ref, acc_ref, *, tq, kc, top_k, idx_bits):
    i = pl.program_id(1)
    t0 = i * tq
    n_kc = (t0 + tq + kc - 1) // kc
    row = lax.broadcasted_iota(jnp.int32, (kc, tq), 0)
    t_idx = t0 + lax.broadcasted_iota(jnp.int32, (kc, tq), 1)

    def score_chunk(c, carry):
        lo8, hi8 = carry
        k_rows = ki_ref[0, pl.ds(pl.multiple_of(c * kc, kc), kc), :]
        acc = jnp.zeros((kc, tq), F32)
        for h in range(IDX_HEADS):
            rel = jnp.dot(k_rows, qit_ref[0, h * IDX_DIM:(h + 1) * IDX_DIM, :],
                          preferred_element_type=F32)
            acc = acc + wit_ref[0, h:h + 1, :] * jnp.maximum(rel, 0.0)
        causal = c * kc + row <= t_idx
        sc_ref[c] = jnp.where(causal, acc, -jnp.inf)
        lo8 = jnp.minimum(lo8, _fold8(jnp.where(causal, acc, jnp.inf), jnp.minimum))
        hi8 = jnp.maximum(hi8, _fold8(jnp.where(causal, acc, -jnp.inf), jnp.maximum))
        return lo8, hi8

    lo8, hi8 = lax.fori_loop(0, n_kc, score_chunk, (jnp.full((SUBLANES, tq), jnp.inf, F32),
                                                     jnp.full((SUBLANES, tq), -jnp.inf, F32)))
    row_min = jnp.min(lo8, axis=0, keepdims=True)
    row_max = jnp.max(hi8, axis=0, keepdims=True)

    def count(pred_fn):
        def body(c, cnt):
            return cnt + _fold8(pred_fn(c, sc_ref[c]).astype(jnp.int32), jnp.add)
        part = lax.fori_loop(0, n_kc, body, jnp.zeros((SUBLANES, tq), jnp.int32))
        return part.sum(axis=0, keepdims=True)

    n_causal = t0 + 1 + lax.broadcasted_iota(jnp.int32, (1, tq), 1)
    short = n_causal <= top_k
    max_ties = count(lambda c, sc: sc >= row_max) >= top_k
    thr0 = jnp.where(short, row_min, row_max)
    done0 = (short | max_ties).astype(F32)

    def bisect_cond(st):
        it, _, _, _, done = st
        return (it < MAX_BISECT) & (jnp.min(done) == 0.0)

    def bisect_step(st):
        it, lo, hi, thr, done = st
        mid = 0.5 * lo + 0.5 * hi
        inside = (mid > lo) & (mid < hi)
        cnt = count(lambda c, sc: sc >= mid)
        active = done == 0.0
        hit = active & inside & (cnt == top_k)
        converged = active & ~inside
        thr = jnp.where(hit, mid, jnp.where(converged, lo, thr))
        done = jnp.where(hit | converged, 1.0, done)
        lo = jnp.where(active & inside & (cnt > top_k), mid, lo)
        hi = jnp.where(active & inside & (cnt < top_k), mid, hi)
        return it + 1, lo, hi, thr, done

    def bisect_body(st):
        for _ in range(BISECT_UNROLL):
            st = bisect_step(st)
        return st

    _, _, _, thr, done = lax.while_loop(bisect_cond, bisect_body,
                                        (jnp.int32(0), row_min, row_max, thr0, done0))

    def to_key(x):
        bits = pltpu.bitcast(x, jnp.int32)
        return bits ^ ((bits >> 31) & 0x7FFFFFFF)

    def exact_kth(_):
        lo = jnp.where(count(lambda c, sc: to_key(sc) >= 0) >= top_k, 0, INT_MIN).astype(jnp.int32)

        def bit_body(b, lo):
            cand = lo + jnp.left_shift(jnp.int32(1), 30 - b)
            return jnp.where(count(lambda c, sc: to_key(sc) >= cand) >= top_k, cand, lo)

        kth = lax.fori_loop(0, 31, bit_body, lo)
        kth = pltpu.bitcast(kth ^ ((kth >> 31) & 0x7FFFFFFF), F32)
        return jnp.where(done == 0.0, kth, thr)

    thr = lax.cond(jnp.min(done) == 0.0, exact_kth, lambda _: thr, 0)

    def bias_chunk(c, cnt):
        sel = sc_ref[c] >= thr
        bias_ref[c] = jnp.where(sel, 0.0, MASKED)
        return cnt + _fold8(sel.astype(jnp.int32), jnp.add)

    n_sel = lax.fori_loop(0, n_kc, bias_chunk, jnp.zeros((SUBLANES, tq), jnp.int32))
    n_sel = n_sel.sum(axis=0, keepdims=True)

    @pl.when(jnp.max(n_sel.astype(F32)) > top_k)
    def _():
        need = top_k - count(lambda c, sc: sc > thr)

        def cut_body(b, p):
            cand = p + jnp.left_shift(jnp.int32(1), idx_bits - 1 - b)
            below = count(lambda c, sc: (sc == thr) & (c * kc + row < cand))
            return jnp.where(below < need, cand, p)

        cut = lax.fori_loop(0, idx_bits, cut_body, jnp.zeros((1, tq), jnp.int32))

        def rebias_chunk(c, carry):
            sc = sc_ref[c]
            sel = (sc > thr) | ((sc == thr) & (c * kc + row <= cut))
            bias_ref[c] = jnp.where(sel, 0.0, MASKED)
            return carry

        lax.fori_loop(0, n_kc, rebias_chunk, 0)

    m_ref[...] = jnp.full(m_ref.shape, MASKED, F32)
    acc_ref[...] = jnp.zeros(acc_ref.shape, F32)

    def attn_chunk(c, carry):
        bias = bias_ref[c]
        rows = pl.ds(pl.multiple_of(c * kc, kc), kc)
        for n in range(N_KV_HEADS):
            k_c = k_ref[0, rows, n * HEAD_DIM:(n + 1) * HEAD_DIM]
            s_all = jnp.dot(k_c, qg_ref[0, n], preferred_element_type=F32)
            ps, alphas = [], []
            for g in range(GROUP):
                h = n * GROUP + g
                s = s_all[:, g * tq:(g + 1) * tq] + bias
                m_old = m_ref[h]
                m_new = jnp.maximum(m_old, jnp.max(s, axis=0, keepdims=True))
                ps.append(jnp.exp2((s - m_new).astype(BF16)))
                alphas.append(jnp.exp2(m_old - m_new))
                m_ref[h] = m_new
            p = jnp.concatenate(ps, axis=1)
            alpha = jnp.concatenate(alphas, axis=1)
            acc_ref[n] = alpha * acc_ref[n] + jnp.dot(
                vt_ref[0, c, n], p, preferred_element_type=F32)
        return carry

    lax.fori_loop(0, n_kc, attn_chunk, 0)
    for h in range(N_HEADS):
        n, g = divmod(h, GROUP)
        cols = slice(g * tq, (g + 1) * tq)
        o_t = acc_ref[n, :HEAD_DIM, cols] / acc_ref[n, HEAD_DIM:HEAD_DIM + 1, cols]
        o_ref[0, :, h * HEAD_DIM:(h + 1) * HEAD_DIM] = o_t.T.astype(BF16)


def _dsa(qit, wit, ki, qg, k, vt, *, tq, kc):
    b, s, _ = k.shape
    top_k = min(TOPK_MAX, s // 4)
    n_chunks = s // kc
    n_qb = s // tq
    v_rows = vt.shape[3]
    per_batch = pl.Buffered(1)
    return pl.pallas_call(
        functools.partial(_dsa_kernel, tq=tq, kc=kc, top_k=top_k, idx_bits=(s - 1).bit_length()),
        grid=(b, s // tq),
        in_specs=[
            pl.BlockSpec((1, IDX_Q_W, tq), lambda bb, i: (bb, 0, i)),
            pl.BlockSpec((1, IDX_HEADS, tq), lambda bb, i: (bb, 0, i)),
            pl.BlockSpec((1, s, IDX_DIM), lambda bb, i: (bb, 0, 0), pipeline_mode=per_batch),
            pl.BlockSpec((1, N_KV_HEADS, HEAD_DIM, GROUP * tq), lambda bb, i: (bb * n_qb + i, 0, 0, 0)),
            pl.BlockSpec((1, s, KV_W), lambda bb, i: (bb, 0, 0), pipeline_mode=per_batch),
            pl.BlockSpec((1, n_chunks, N_KV_HEADS, v_rows, kc), lambda bb, i: (bb, 0, 0, 0, 0),
                         pipeline_mode=per_batch),
        ],
        out_specs=pl.BlockSpec((1, tq, Q_W), lambda bb, i: (bb, i, 0)),
        out_shape=jax.ShapeDtypeStruct((b, s, Q_W), BF16),
        scratch_shapes=[
            pltpu.VMEM((n_chunks, kc, tq), F32),
            pltpu.VMEM((n_chunks, kc, tq), F32),
            pltpu.VMEM((N_HEADS, 1, tq), F32),
            pltpu.VMEM((N_KV_HEADS, v_rows, GROUP * tq), F32),
        ],
        compiler_params=pltpu.CompilerParams(
            dimension_semantics=("arbitrary", "arbitrary"), vmem_limit_bytes=VMEM_LIMIT),
        name="dsa",
    )(qit, wit, ki, qg, k, vt)


def _out_proj_kernel(a_ref, w_ref, x_ref, o_ref):
    o_ref[...] = x_ref[...] + jnp.dot(a_ref[...], w_ref[...], preferred_element_type=F32)


def _out_proj(a2d, w, x2d, *, layer, tm=512):
    t, k = a2d.shape
    n = w.shape[2]
    return pl.pallas_call(
        _out_proj_kernel,
        grid=(t // tm,),
        in_specs=[
            pl.BlockSpec((tm, k), lambda i: (i, 0)),
            pl.BlockSpec((None, k, n), lambda i: (layer, 0, 0), pipeline_mode=pl.Buffered(1)),
            pl.BlockSpec((tm, n), lambda i: (i, 0)),
        ],
        out_specs=pl.BlockSpec((tm, n), lambda i: (i, 0)),
        out_shape=jax.ShapeDtypeStruct((t, n), F32),
        compiler_params=pltpu.CompilerParams(
            dimension_semantics=("arbitrary",), vmem_limit_bytes=VMEM_LIMIT),
        name="out_proj",
    )(a2d, w, x2d)


def _conv_kernel(x_ref, g_ref, wb_ref, wc_ref, wu_ref, cw_ref, wo_ref, o_ref, h_ref, carry_ref,
                 *, tm, tiles_per_seq):
    i = pl.program_id(0)
    j = pl.program_id(1)

    def mixer_chunk():
        h = h_ref[...]
        b_gate = jnp.dot(h, wb_ref[...], preferred_element_type=F32)
        c_gate = jnp.dot(h, wc_ref[...], preferred_element_type=F32)
        u = jnp.dot(h, wu_ref[...], preferred_element_type=F32)
        z = c_gate * u
        prev = jnp.where(i % tiles_per_seq == 0, 0.0, carry_ref[j])
        carry_ref[j] = z[tm - SUBLANES:, :]
        r = lax.broadcasted_iota(jnp.int32, z.shape, 0)
        z1 = jnp.where(r == 0, prev[7:8, :], pltpu.roll(z, 1, 0))
        z2 = jnp.where(r == 0, prev[6:7, :], jnp.where(r == 1, prev[7:8, :], pltpu.roll(z, 2, 0)))
        y = z2 * cw_ref[0:1, :] + z1 * cw_ref[1:2, :] + z * cw_ref[2:3, :]
        gated = (b_gate * y).astype(BF16)
        return jnp.dot(gated, wo_ref[...], preferred_element_type=F32)

    @pl.when(j == 0)
    def _():
        x = x_ref[...]
        h_ref[...] = _rmsnorm_rows(x, g_ref[...]).astype(BF16)
        o_ref[...] = x + mixer_chunk()

    @pl.when(j > 0)
    def _():
        o_ref[...] += mixer_chunk()


def _conv_mixer(x2d, g, w_in, cw, wo, *, layer, seq, tm=512, tn=512):
    t = x2d.shape[0]
    n_chunks = D_MODEL // tn
    return pl.pallas_call(
        functools.partial(_conv_kernel, tm=tm, tiles_per_seq=seq // tm),
        grid=(t // tm, n_chunks),
        in_specs=[
            pl.BlockSpec((tm, D_MODEL), lambda i, j: (i, 0)),
            pl.BlockSpec((1, D_MODEL), lambda i, j: (0, 0)),
            pl.BlockSpec((None, D_MODEL, tn), lambda i, j: (layer, 0, j)),
            pl.BlockSpec((None, D_MODEL, tn), lambda i, j: (layer, 0, j + n_chunks)),
            pl.BlockSpec((None, D_MODEL, tn), lambda i, j: (layer, 0, j + 2 * n_chunks)),
            pl.BlockSpec((None, 3, tn), lambda i, j: (layer, 0, j)),
            pl.BlockSpec((None, tn, D_MODEL), lambda i, j: (layer, j, 0)),
        ],
        out_specs=pl.BlockSpec((tm, D_MODEL), lambda i, j: (i, 0)),
        out_shape=jax.ShapeDtypeStruct((t, D_MODEL), F32),
        scratch_shapes=[
            pltpu.VMEM((tm, D_MODEL), BF16),
            pltpu.VMEM((n_chunks, SUBLANES, tn), F32),
        ],
        compiler_params=pltpu.CompilerParams(
            dimension_semantics=("arbitrary", "arbitrary"), vmem_limit_bytes=VMEM_LIMIT),
        name="conv_mixer",
    )(x2d, g, w_in, w_in, w_in, cw, wo)


def _mlp_kernel(x_ref, g_ref, w1_ref, w2_ref, o_ref, h_ref):
    j = pl.program_id(1)

    def hidden_chunk():
        a = jnp.maximum(jnp.dot(h_ref[...], w1_ref[...], preferred_element_type=F32), 0.0)
        return jnp.dot((a * a).astype(BF16), w2_ref[...], preferred_element_type=F32)

    @pl.when(j == 0)
    def _():
        x = x_ref[...]
        h_ref[...] = _rmsnorm_rows(x, g_ref[...]).astype(BF16)
        o_ref[...] = x + hidden_chunk()

    @pl.when(j > 0)
    def _():
        o_ref[...] += hidden_chunk()


def _mlp(x2d, g, w1, w2, *, layer, tm=1024, tf=512):
    t = x2d.shape[0]
    return pl.pallas_call(
        _mlp_kernel,
        grid=(t // tm, D_FF // tf),
        in_specs=[
            pl.BlockSpec((tm, D_MODEL), lambda i, j: (i, 0)),
            pl.BlockSpec((1, D_MODEL), lambda i, j: (0, 0)),
            pl.BlockSpec((None, D_MODEL, tf), lambda i, j: (layer, 0, j)),
            pl.BlockSpec((None, tf, D_MODEL), lambda i, j: (layer, j, 0)),
        ],
        out_specs=pl.BlockSpec((tm, D_MODEL), lambda i, j: (i, 0)),
        out_shape=jax.ShapeDtypeStruct((t, D_MODEL), F32),
        scratch_shapes=[pltpu.VMEM((tm, D_MODEL), BF16)],
        compiler_params=pltpu.CompilerParams(
            dimension_semantics=("arbitrary", "arbitrary"), vmem_limit_bytes=VMEM_LIMIT),
        name="mlp",
    )(x2d, g, w1, w2)


def _rope_inv_lanes():
    def inv(rot_dim):
        return ROPE_THETA ** (-jnp.arange(0, rot_dim, 2, dtype=F32) / rot_dim)
    inv_q, inv_i = jnp.tile(inv(ROPE_DIM), 2), jnp.tile(inv(IDX_ROPE_DIM), 2)
    gap = jnp.zeros((IDX_DIM - IDX_ROPE_DIM,), F32)
    tail = jnp.zeros((LANES - IDX_LANE_SHIFT - IDX_DIM - IDX_ROPE_DIM,), F32)
    lanes = jnp.concatenate([inv_q, inv_i, gap, inv_i, tail])
    assert lanes.shape == (LANES,) and IDX_LANE_SHIFT == inv_q.shape[0]
    return lanes[None, :]


def _dsa_layer(x2d, pos2d, batch, seq, g, w_in, w_kw, q_g, k_g, w_out, layer, *, tq=512, kc=512):
    qg, k, vt, qit, ki, wit = _attn_in(
        x2d, pos2d, g[None, :], w_in, w_kw, q_g[None, :], k_g[None, :], _rope_inv_lanes(),
        layer=layer, batch=batch, seq=seq, tq=tq, kc=kc)
    o = _dsa(qit, wit, ki.reshape(batch, seq, IDX_DIM), qg, k.reshape(batch, seq, KV_W),
             vt.reshape((batch, seq // kc) + vt.shape[1:]), tq=tq, kc=kc)
    return _out_proj(o.reshape(batch * seq, Q_W), w_out, x2d, layer=layer)


def kernel(x, positions, attn_norm_g, attn_w_in, attn_q_norm_g, attn_k_norm_g, attn_w_out,
           conv_norm_g, conv_w_in, conv_w, conv_w_out, mlp_norm_g, mlp_w1, mlp_w2):
    batch, seq, d = x.shape
    depth = mlp_w1.shape[0]
    x2d = x.reshape(batch * seq, d)
    pos2d = positions.reshape(batch * seq, 1)
    attn_w_in_b, attn_w_out_b = attn_w_in.astype(BF16), attn_w_out.astype(BF16)
    attn_w_kw_b = jnp.pad(attn_w_in[:, :, MAIN_W:],
                          ((0, 0), (0, 0), (0, LANES - (IDX_DIM + IDX_HEADS)))).astype(BF16)
    conv_w_in_b, conv_w_out_b = conv_w_in.astype(BF16), conv_w_out.astype(BF16)
    mlp_w1_b, mlp_w2_b = mlp_w1.astype(BF16), mlp_w2.astype(BF16)
    for i in range(depth):
        j = i // 2
        if i % 2 == 0:
            x2d = _dsa_layer(x2d, pos2d, batch, seq, attn_norm_g[j], attn_w_in_b, attn_w_kw_b,
                             attn_q_norm_g[j], attn_k_norm_g[j], attn_w_out_b, j)
        else:
            x2d = _conv_mixer(x2d, conv_norm_g[j][None, :], conv_w_in_b, conv_w, conv_w_out_b,
                              layer=j, seq=seq)
        x2d = _mlp(x2d, mlp_norm_g[i][None, :], mlp_w1_b, mlp_w2_b, layer=i)
    return x2d.reshape(batch, seq, d)
```

```python
import functools
import math

import jax
import jax.numpy as jnp
from jax import lax
from jax.experimental import pallas as pl
from jax.experimental.pallas import tpu as pltpu

D_MODEL = 2048
N_HEADS = 16
N_KV_HEADS = 4
HEAD_DIM = 128
GROUP = N_HEADS // N_KV_HEADS
ROPE_DIM = HEAD_DIM // 4
ROPE_THETA = 500000.0
IDX_HEADS = 16
IDX_DIM = 64
IDX_ROPE_DIM = IDX_DIM // 4
TOPK_MAX = 256
Q_W = N_HEADS * HEAD_DIM
KV_W = N_KV_HEADS * HEAD_DIM
IDX_Q_W = IDX_HEADS * IDX_DIM
MAIN_W = Q_W + 2 * KV_W + IDX_Q_W
D_FF = 4 * D_MODEL
EPS = 1e-6
LANES = 128
SUBLANES = 8
VMEM_LIMIT = 56 * 1024 * 1024

INT_MIN = -(2 ** 31)
MASKED = -1e30
MAX_BISECT = 40
BISECT_UNROLL = 2
Q_SCALE_LOG2E = HEAD_DIM ** -0.5 * math.log2(math.e)

F32 = jnp.float32
BF16 = jnp.bfloat16


def _rmsnorm_rows(x, g):
    ms = jnp.mean(x * x, axis=-1, keepdims=True)
    return (x * lax.rsqrt(ms + EPS)) * g


def _rope_tables(cos, sin, period, half):
    d = lax.broadcasted_iota(jnp.int32, cos.shape, 1) & (period - 1)
    c = jnp.where(d < 2 * half, cos, 1.0)
    s_lo = jnp.where(d < half, -sin, 0.0)
    s_hi = jnp.where((d >= half) & (d < 2 * half), sin, 0.0)
    return c, s_lo, s_hi


def _rope(x, c, s_lo, s_hi, half):
    return x * c + pltpu.roll(x, LANES - half, 1) * s_lo + pltpu.roll(x, half, 1) * s_hi


V_ONES_ROWS = 2 * SUBLANES
IDX_LANE_SHIFT = ROPE_DIM


def _attn_in_kernel(x_ref, pos_ref, g_ref, w_ref, wkw_ref, qgain_ref, kgain_ref, inv_ref,
                    qg_ref, k_ref, vt_ref, qit_ref, ki_ref, wit_ref, h_ref, acc_ref, tq_ref, ti_ref,
                    *, tn, tq, kc):
    j = pl.program_id(1)
    tm = x_ref.shape[0]
    n_q = Q_W // tn
    n_tiles = MAIN_W // tn

    def project(slot):
        acc_ref[slot] = jnp.dot(h_ref[...], w_ref[...], preferred_element_type=F32)

    def prologue():
        h = _rmsnorm_rows(x_ref[...], g_ref[...]).astype(BF16)
        h_ref[...] = h
        ang = pos_ref[...].astype(F32) * inv_ref[...]
        cos, sin = jnp.cos(ang), jnp.sin(ang)
        cq, sq_lo, sq_hi = _rope_tables(cos, sin, HEAD_DIM, ROPE_DIM // 2)
        tq_ref[0], tq_ref[1], tq_ref[2] = cq, sq_lo, sq_hi
        back = LANES - IDX_LANE_SHIFT
        ci, si_lo, si_hi = _rope_tables(pltpu.roll(cos, back, 1), pltpu.roll(sin, back, 1),
                                        IDX_DIM, IDX_ROPE_DIM // 2)
        ti_ref[0], ti_ref[1], ti_ref[2] = ci, si_lo, si_hi
        kw = jnp.dot(h, wkw_ref[...], preferred_element_type=F32)
        roped = _rope(kw, ci, si_lo, si_hi, IDX_ROPE_DIM // 2)
        ki_ref[...] = roped[:, :IDX_DIM].astype(BF16)
        scale = IDX_HEADS ** -0.5 * IDX_DIM ** -0.5
        wit_ref[0] = (kw * scale).T[IDX_DIM:IDX_DIM + IDX_HEADS, :]

    def head_norm_rope(acc, hh, gain_ref):
        y = _rmsnorm_rows(acc[:, hh * HEAD_DIM:(hh + 1) * HEAD_DIM], gain_ref[...])
        return _rope(y, tq_ref[0], tq_ref[1], tq_ref[2], ROPE_DIM // 2)

    def finish(tile, slot):
        acc = acc_ref[slot]
        if tile < n_q:
            for g in range(GROUP):
                y = head_norm_rope(acc, g, qgain_ref) * Q_SCALE_LOG2E
                for r in range(tm // tq):
                    qg_ref[r, 0, :, g * tq:(g + 1) * tq] = y[r * tq:(r + 1) * tq, :].T.astype(BF16)
        elif tile == n_q:
            for hh in range(N_KV_HEADS):
                k_ref[:, hh * HEAD_DIM:(hh + 1) * HEAD_DIM] = (
                    head_norm_rope(acc, hh, kgain_ref).astype(BF16))
        elif tile == n_q + 1:
            for n in range(N_KV_HEADS):
                for cc in range(tm // kc):
                    v = acc[cc * kc:(cc + 1) * kc, n * HEAD_DIM:(n + 1) * HEAD_DIM]
                    vt_ref[cc, n, :HEAD_DIM, :] = v.T.astype(BF16)
                    vt_ref[cc, n, HEAD_DIM:, :] = jnp.ones((V_ONES_ROWS, kc), BF16)
        else:
            for hh in range(tn // LANES):
                sl = slice(hh * LANES, (hh + 1) * LANES)
                y = _rope(acc[:, sl], ti_ref[0], ti_ref[1], ti_ref[2], IDX_ROPE_DIM // 2)
                qit_ref[0, sl, :] = y.T.astype(BF16)

    for step in range(n_tiles + 1):
        @pl.when(j == step)
        def _(step=step):
            if step == 0:
                prologue()
            if step < n_tiles:
                project(step % 2)
            if step > 0:
                finish(step - 1, (step - 1) % 2)


def _attn_in(x2d, pos2d, g, w_in, w_kw, q_g, k_g, inv, *, layer, batch, seq, tq, kc, tm=1024, tn=512):
    t = x2d.shape[0]
    assert tn == KV_W == GROUP * HEAD_DIM and tm % tq == 0 and tm % kc == 0 and seq % tm == 0
    tiles_per_seq = seq // tm
    n_q = Q_W // tn
    n_tiles = MAIN_W // tn
    first_qi = n_tiles - IDX_Q_W // tn
    return pl.pallas_call(
        functools.partial(_attn_in_kernel, tn=tn, tq=tq, kc=kc),
        grid=(t // tm, n_tiles + 1),
        in_specs=[
            pl.BlockSpec((tm, D_MODEL), lambda i, j: (i, 0)),
            pl.BlockSpec((tm, 1), lambda i, j: (i, 0)),
            pl.BlockSpec((1, D_MODEL), lambda i, j: (0, 0)),
            pl.BlockSpec((None, D_MODEL, tn), lambda i, j: (layer, 0, jnp.minimum(j, n_tiles - 1))),
            pl.BlockSpec((None, D_MODEL, LANES), lambda i, j: (layer, 0, 0)),
            pl.BlockSpec((1, HEAD_DIM), lambda i, j: (0, 0)),
            pl.BlockSpec((1, HEAD_DIM), lambda i, j: (0, 0)),
            pl.BlockSpec((1, LANES), lambda i, j: (0, 0)),
        ],
        out_specs=[
            pl.BlockSpec((tm // tq, 1, HEAD_DIM, GROUP * tq),
                         lambda i, j: (i, jnp.clip(j - 1, 0, n_q - 1), 0, 0)),
            pl.BlockSpec((tm, KV_W), lambda i, j: (i, 0)),
            pl.BlockSpec((tm // kc, N_KV_HEADS, HEAD_DIM + V_ONES_ROWS, kc), lambda i, j: (i, 0, 0, 0)),
            pl.BlockSpec((1, tn, tm),
                         lambda i, j: (i // tiles_per_seq, jnp.maximum(j - 1 - first_qi, 0),
                                       i % tiles_per_seq)),
            pl.BlockSpec((tm, IDX_DIM), lambda i, j: (i, 0)),
            pl.BlockSpec((1, IDX_HEADS, tm), lambda i, j: (i // tiles_per_seq, 0, i % tiles_per_seq)),
        ],
        out_shape=[
            jax.ShapeDtypeStruct((t // tq, N_KV_HEADS, HEAD_DIM, GROUP * tq), BF16),
            jax.ShapeDtypeStruct((t, KV_W), BF16),
            jax.ShapeDtypeStruct((t // kc, N_KV_HEADS, HEAD_DIM + V_ONES_ROWS, kc), BF16),
            jax.ShapeDtypeStruct((batch, IDX_Q_W, seq), BF16),
            jax.ShapeDtypeStruct((t, IDX_DIM), BF16),
            jax.ShapeDtypeStruct((batch, IDX_HEADS, seq), F32),
        ],
        scratch_shapes=[
            pltpu.VMEM((tm, D_MODEL), BF16),
            pltpu.VMEM((2, tm, tn), F32),
            pltpu.VMEM((3, tm, LANES), F32),
            pltpu.VMEM((3, tm, LANES), F32),
        ],
        compiler_params=pltpu.CompilerParams(
            dimension_semantics=("arbitrary", "arbitrary"), vmem_limit_bytes=VMEM_LIMIT),
        name="attn_in",
    )(x2d, pos2d, g, w_in, w_kw, q_g, k_g, inv)


def _fold8(x, op):
    rows, cols = x.shape
    x = x.reshape(rows // SUBLANES, SUBLANES, cols)
    if op is jnp.add:
        return x.sum(axis=0)
    return x.min(axis=0) if op is jnp.minimum else x.max(axis=0)


def _dsa_kernel(qit_ref, wit_ref, ki_ref, qg_ref, k_ref, vt_ref, o_ref, sc_ref, bias_ref,
                m_ref, acc_ref, *, tq, kc, top_k, idx_bits):
    i = pl.program_id(1)
    t0 = i * tq
    n_kc = (t0 + tq + kc - 1) // kc
    row = lax.broadcasted_iota(jnp.int32, (kc, tq), 0)
    t_idx = t0 + lax.broadcasted_iota(jnp.int32, (kc, tq), 1)

    def score_chunk(c, carry):
        lo8, hi8 = carry
        k_rows = ki_ref[0, pl.ds(pl.multiple_of(c * kc, kc), kc), :]
        acc = jnp.zeros((kc, tq), F32)
        for h in range(IDX_HEADS):
            rel = jnp.dot(k_rows, qit_ref[0, h * IDX_DIM:(h + 1) * IDX_DIM, :],
                          preferred_element_type=F32)
            acc = acc + wit_ref[0, h:h + 1, :] * jnp.maximum(rel, 0.0)
        causal = c * kc + row <= t_idx
        sc_ref[c] = jnp.where(causal, acc, -jnp.inf)
        lo8 = jnp.minimum(lo8, _fold8(jnp.where(causal, acc, jnp.inf), jnp.minimum))
        hi8 = jnp.maximum(hi8, _fold8(jnp.where(causal, acc, -jnp.inf), jnp.maximum))
        return lo8, hi8

    lo8, hi8 = lax.fori_loop(0, n_kc, score_chunk, (jnp.full((SUBLANES, tq), jnp.inf, F32),
                                                     jnp.full((SUBLANES, tq), -jnp.inf, F32)))
    row_min = jnp.min(lo8, axis=0, keepdims=True)
    row_max = jnp.max(hi8, axis=0, keepdims=True)

    def count(pred_fn):
        def body(c, cnt):
            return cnt + _fold8(pred_fn(c, sc_ref[c]).astype(jnp.int32), jnp.add)
        part = lax.fori_loop(0, n_kc, body, jnp.zeros((SUBLANES, tq), jnp.int32))
        return part.sum(axis=0, keepdims=True)

    n_causal = t0 + 1 + lax.broadcasted_iota(jnp.int32, (1, tq), 1)
    short = n_causal <= top_k
    done0 = short.astype(F32)
    above_max = row_max + jnp.maximum(jnp.abs(row_max) * 2.0 ** -20, jnp.finfo(F32).tiny)

    def bisect_cond(st):
        it, _, _, _, done = st
        return (it < MAX_BISECT) & (jnp.min(done) == 0.0)

    def bisect_step(st):
        it, lo, hi, thr, done = st
        mid = 0.5 * lo + 0.5 * hi
        inside = (mid > lo) & (mid < hi)
        cnt = count(lambda c, sc: sc >= mid)
        active = done == 0.0
        hit = active & inside & (cnt == top_k)
        converged = active & ~inside
        thr = jnp.where(hit, mid, jnp.where(converged, lo, thr))
        done = jnp.where(hit | converged, 1.0, done)
        lo = jnp.where(active & inside & (cnt > top_k), mid, lo)
        hi = jnp.where(active & inside & (cnt < top_k), mid, hi)
        return it + 1, lo, hi, thr, done

    def bisect_body(st):
        for _ in range(BISECT_UNROLL):
            st = bisect_step(st)
        return st

    _, _, _, thr, done = lax.while_loop(bisect_cond, bisect_body,
                                        (jnp.int32(0), row_min, above_max, row_min, done0))

    def to_key(x):
        bits = pltpu.bitcast(x, jnp.int32)
        return bits ^ ((bits >> 31) & 0x7FFFFFFF)

    def exact_kth(_):
        lo = jnp.where(count(lambda c, sc: to_key(sc) >= 0) >= top_k, 0, INT_MIN).astype(jnp.int32)

        def bit_body(b, lo):
            cand = lo + jnp.left_shift(jnp.int32(1), 30 - b)
            return jnp.where(count(lambda c, sc: to_key(sc) >= cand) >= top_k, cand, lo)

        kth = lax.fori_loop(0, 31, bit_body, lo)
        kth = pltpu.bitcast(kth ^ ((kth >> 31) & 0x7FFFFFFF), F32)
        return jnp.where(done == 0.0, kth, thr)

    thr = lax.cond(jnp.min(done) == 0.0, exact_kth, lambda _: thr, 0)

    def bias_chunk(c, cnt):
        sel = sc_ref[c] >= thr
        bias_ref[c] = jnp.where(sel, 0.0, MASKED)
        return cnt + _fold8(sel.astype(jnp.int32), jnp.add)

    n_sel = lax.fori_loop(0, n_kc, bias_chunk, jnp.zeros((SUBLANES, tq), jnp.int32))
    n_sel = n_sel.sum(axis=0, keepdims=True)

    @pl.when(jnp.max(n_sel.astype(F32)) > top_k)
    def _():
        need = top_k - count(lambda c, sc: sc > thr)

        def cut_body(b, p):
            cand = p + jnp.left_shift(jnp.int32(1), idx_bits - 1 - b)
            below = count(lambda c, sc: (sc == thr) & (c * kc + row < cand))
            return jnp.where(below < need, cand, p)

        cut = lax.fori_loop(0, idx_bits, cut_body, jnp.zeros((1, tq), jnp.int32))

        def rebias_chunk(c, carry):
            sc = sc_ref[c]
            sel = (sc > thr) | ((sc == thr) & (c * kc + row <= cut))
            bias_ref[c] = jnp.where(sel, 0.0, MASKED)
            return carry

        lax.fori_loop(0, n_kc, rebias_chunk, 0)

    m_ref[...] = jnp.full(m_ref.shape, MASKED, F32)
    acc_ref[...] = jnp.zeros(acc_ref.shape, F32)

    def attn_chunk(c, carry):
        bias = bias_ref[c]
        rows = pl.ds(pl.multiple_of(c * kc, kc), kc)
        for n in range(N_KV_HEADS):
            k_c = k_ref[0, rows, n * HEAD_DIM:(n + 1) * HEAD_DIM]
            s_all = jnp.dot(k_c, qg_ref[0, n], preferred_element_type=F32)
            ps, alphas = [], []
            for g in range(GROUP):
                h = n * GROUP + g
                s = s_all[:, g * tq:(g + 1) * tq] + bias
                m_old = m_ref[h]
                m_new = jnp.maximum(m_old, jnp.max(s, axis=0, keepdims=True))
                ps.append(jnp.exp2((s - m_new).astype(BF16)))
                alphas.append(jnp.exp2(m_old - m_new))
                m_ref[h] = m_new
            p = jnp.concatenate(ps, axis=1)
            alpha = jnp.concatenate(alphas, axis=1)
            acc_ref[n] = alpha * acc_ref[n] + jnp.dot(
                vt_ref[0, c, n], p, preferred_element_type=F32)
        return carry

    lax.fori_loop(0, n_kc, attn_chunk, 0)
    for h in range(N_HEADS):
        n, g = divmod(h, GROUP)
        cols = slice(g * tq, (g + 1) * tq)
        o_t = acc_ref[n, :HEAD_DIM, cols] / acc_ref[n, HEAD_DIM:HEAD_DIM + 1, cols]
        o_ref[0, :, h * HEAD_DIM:(h + 1) * HEAD_DIM] = o_t.T.astype(BF16)


def _dsa(qit, wit, ki, qg, k, vt, *, tq, kc):
    b, s, _ = k.shape
    top_k = min(TOPK_MAX, s // 4)
    n_chunks = s // kc
    n_qb = s // tq
    v_rows = vt.shape[3]
    per_batch = pl.Buffered(1)
    return pl.pallas_call(
        functools.partial(_dsa_kernel, tq=tq, kc=kc, top_k=top_k, idx_bits=(s - 1).bit_length()),
        grid=(b, s // tq),
        in_specs=[
            pl.BlockSpec((1, IDX_Q_W, tq), lambda bb, i: (bb, 0, i)),
            pl.BlockSpec((1, IDX_HEADS, tq), lambda bb, i: (bb, 0, i)),
            pl.BlockSpec((1, s, IDX_DIM), lambda bb, i: (bb, 0, 0), pipeline_mode=per_batch),
            pl.BlockSpec((1, N_KV_HEADS, HEAD_DIM, GROUP * tq), lambda bb, i: (bb * n_qb + i, 0, 0, 0)),
            pl.BlockSpec((1, s, KV_W), lambda bb, i: (bb, 0, 0), pipeline_mode=per_batch),
            pl.BlockSpec((1, n_chunks, N_KV_HEADS, v_rows, kc), lambda bb, i: (bb, 0, 0, 0, 0),
                         pipeline_mode=per_batch),
        ],
        out_specs=pl.BlockSpec((1, tq, Q_W), lambda bb, i: (bb, i, 0)),
        out_shape=jax.ShapeDtypeStruct((b, s, Q_W), BF16),
        scratch_shapes=[
            pltpu.VMEM((n_chunks, kc, tq), F32),
            pltpu.VMEM((n_chunks, kc, tq), F32),
            pltpu.VMEM((N_HEADS, 1, tq), F32),
            pltpu.VMEM((N_KV_HEADS, v_rows, GROUP * tq), F32),
        ],
        compiler_params=pltpu.CompilerParams(
            dimension_semantics=("arbitrary", "arbitrary"), vmem_limit_bytes=VMEM_LIMIT),
        name="dsa",
    )(qit, wit, ki, qg, k, vt)


def _out_proj_kernel(a_ref, w_ref, x_ref, o_ref):
    o_ref[...] = x_ref[...] + jnp.dot(a_ref[...], w_ref[...], preferred_element_type=F32)


def _out_proj(a2d, w, x2d, *, layer, tm=512):
    t, k = a2d.shape
    n = w.shape[2]
    return pl.pallas_call(
        _out_proj_kernel,
        grid=(t // tm,),
        in_specs=[
            pl.BlockSpec((tm, k), lambda i: (i, 0)),
            pl.BlockSpec((None, k, n), lambda i: (layer, 0, 0), pipeline_mode=pl.Buffered(1)),
            pl.BlockSpec((tm, n), lambda i: (i, 0)),
        ],
        out_specs=pl.BlockSpec((tm, n), lambda i: (i, 0)),
        out_shape=jax.ShapeDtypeStruct((t, n), F32),
        compiler_params=pltpu.CompilerParams(
            dimension_semantics=("arbitrary",), vmem_limit_bytes=VMEM_LIMIT),
        name="out_proj",
    )(a2d, w, x2d)


def _conv_kernel(x_ref, g_ref, wb_ref, wc_ref, wu_ref, cw_ref, wo_ref, o_ref, h_ref, carry_ref,
                 *, tm, tiles_per_seq):
    i = pl.program_id(0)
    j = pl.program_id(1)

    def mixer_chunk():
        h = h_ref[...]
        b_gate = jnp.dot(h, wb_ref[...], preferred_element_type=F32)
        c_gate = jnp.dot(h, wc_ref[...], preferred_element_type=F32)
        u = jnp.dot(h, wu_ref[...], preferred_element_type=F32)
        z = c_gate * u
        prev = jnp.where(i % tiles_per_seq == 0, 0.0, carry_ref[j])
        carry_ref[j] = z[tm - SUBLANES:, :]
        r = lax.broadcasted_iota(jnp.int32, z.shape, 0)
        z1 = jnp.where(r == 0, prev[7:8, :], pltpu.roll(z, 1, 0))
        z2 = jnp.where(r == 0, prev[6:7, :], jnp.where(r == 1, prev[7:8, :], pltpu.roll(z, 2, 0)))
        y = z2 * cw_ref[0:1, :] + z1 * cw_ref[1:2, :] + z * cw_ref[2:3, :]
        gated = (b_gate * y).astype(BF16)
        return jnp.dot(gated, wo_ref[...], preferred_element_type=F32)

    @pl.when(j == 0)
    def _():
        x = x_ref[...]
        h_ref[...] = _rmsnorm_rows(x, g_ref[...]).astype(BF16)
        o_ref[...] = x + mixer_chunk()

    @pl.when(j > 0)
    def _():
        o_ref[...] += mixer_chunk()


def _conv_mixer(x2d, g, w_in, cw, wo, *, layer, seq, tm=512, tn=512):
    t = x2d.shape[0]
    n_chunks = D_MODEL // tn
    return pl.pallas_call(
        functools.partial(_conv_kernel, tm=tm, tiles_per_seq=seq // tm),
        grid=(t // tm, n_chunks),
        in_specs=[
            pl.BlockSpec((tm, D_MODEL), lambda i, j: (i, 0)),
            pl.BlockSpec((1, D_MODEL), lambda i, j: (0, 0)),
            pl.BlockSpec((None, D_MODEL, tn), lambda i, j: (layer, 0, j)),
            pl.BlockSpec((None, D_MODEL, tn), lambda i, j: (layer, 0, j + n_chunks)),
            pl.BlockSpec((None, D_MODEL, tn), lambda i, j: (layer, 0, j + 2 * n_chunks)),
            pl.BlockSpec((None, 3, tn), lambda i, j: (layer, 0, j)),
            pl.BlockSpec((None, tn, D_MODEL), lambda i, j: (layer, j, 0)),
        ],
        out_specs=pl.BlockSpec((tm, D_MODEL), lambda i, j: (i, 0)),
        out_shape=jax.ShapeDtypeStruct((t, D_MODEL), F32),
        scratch_shapes=[
            pltpu.VMEM((tm, D_MODEL), BF16),
            pltpu.VMEM((n_chunks, SUBLANES, tn), F32),
        ],
        compiler_params=pltpu.CompilerParams(
            dimension_semantics=("arbitrary", "arbitrary"), vmem_limit_bytes=VMEM_LIMIT),
        name="conv_mixer",
    )(x2d, g, w_in, w_in, w_in, cw, wo)


def _mlp_kernel(x_ref, g_ref, w1_ref, w2_ref, o_ref, h_ref):
    j = pl.program_id(1)

    def hidden_chunk():
        a = jnp.maximum(jnp.dot(h_ref[...], w1_ref[...], preferred_element_type=F32), 0.0)
        return jnp.dot((a * a).astype(BF16), w2_ref[...], preferred_element_type=F32)

    @pl.when(j == 0)
    def _():
        x = x_ref[...]
        h_ref[...] = _rmsnorm_rows(x, g_ref[...]).astype(BF16)
        o_ref[...] = x + hidden_chunk()

    @pl.when(j > 0)
    def _():
        o_ref[...] += hidden_chunk()


def _mlp(x2d, g, w1, w2, *, layer, tm=1024, tf=512):
    t = x2d.shape[0]
    return pl.pallas_call(
        _mlp_kernel,
        grid=(t // tm, D_FF // tf),
        in_specs=[
            pl.BlockSpec((tm, D_MODEL), lambda i, j: (i, 0)),
            pl.BlockSpec((1, D_MODEL), lambda i, j: (0, 0)),
            pl.BlockSpec((None, D_MODEL, tf), lambda i, j: (layer, 0, j)),
            pl.BlockSpec((None, tf, D_MODEL), lambda i, j: (layer, j, 0)),
        ],
        out_specs=pl.BlockSpec((tm, D_MODEL), lambda i, j: (i, 0)),
        out_shape=jax.ShapeDtypeStruct((t, D_MODEL), F32),
        scratch_shapes=[pltpu.VMEM((tm, D_MODEL), BF16)],
        compiler_params=pltpu.CompilerParams(
            dimension_semantics=("arbitrary", "arbitrary"), vmem_limit_bytes=VMEM_LIMIT),
        name="mlp",
    )(x2d, g, w1, w2)


def _rope_inv_lanes():
    def inv(rot_dim):
        return ROPE_THETA ** (-jnp.arange(0, rot_dim, 2, dtype=F32) / rot_dim)
    inv_q, inv_i = jnp.tile(inv(ROPE_DIM), 2), jnp.tile(inv(IDX_ROPE_DIM), 2)
    gap = jnp.zeros((IDX_DIM - IDX_ROPE_DIM,), F32)
    tail = jnp.zeros((LANES - IDX_LANE_SHIFT - IDX_DIM - IDX_ROPE_DIM,), F32)
    lanes = jnp.concatenate([inv_q, inv_i, gap, inv_i, tail])
    assert lanes.shape == (LANES,) and IDX_LANE_SHIFT == inv_q.shape[0]
    return lanes[None, :]


def _dsa_layer(x2d, pos2d, batch, seq, g, w_in, w_kw, q_g, k_g, w_out, layer, *, tq=512, kc=512):
    qg, k, vt, qit, ki, wit = _attn_in(
        x2d, pos2d, g[None, :], w_in, w_kw, q_g[None, :], k_g[None, :], _rope_inv_lanes(),
        layer=layer, batch=batch, seq=seq, tq=tq, kc=kc)
    o = _dsa(qit, wit, ki.reshape(batch, seq, IDX_DIM), qg, k.reshape(batch, seq, KV_W),
             vt.reshape((batch, seq // kc) + vt.shape[1:]), tq=tq, kc=kc)
    return _out_proj(o.reshape(batch * seq, Q_W), w_out, x2d, layer=layer)


def kernel(x, positions, attn_norm_g, attn_w_in, attn_q_norm_g, attn_k_norm_g, attn_w_out,
           conv_norm_g, conv_w_in, conv_w, conv_w_out, mlp_norm_g, mlp_w1, mlp_w2):
    batch, seq, d = x.shape
    depth = mlp_w1.shape[0]
    x2d = x.reshape(batch * seq, d)
    pos2d = positions.reshape(batch * seq, 1)
    attn_w_in_b, attn_w_out_b = attn_w_in[:, :, :MAIN_W].astype(BF16), attn_w_out.astype(BF16)
    attn_w_kw_b = jnp.pad(attn_w_in[:, :, MAIN_W:],
                          ((0, 0), (0, 0), (0, LANES - (IDX_DIM + IDX_HEADS)))).astype(BF16)
    conv_w_in_b, conv_w_out_b = conv_w_in.astype(BF16), conv_w_out.astype(BF16)
    mlp_w1_b, mlp_w2_b = mlp_w1.astype(BF16), mlp_w2.astype(BF16)
    for i in range(depth):
        j = i // 2
        if i % 2 == 0:
            x2d = _dsa_layer(x2d, pos2d, batch, seq, attn_norm_g[j], attn_w_in_b, attn_w_kw_b,
                             attn_q_norm_g[j], attn_k_norm_g[j], attn_w_out_b, j)
        else:
            x2d = _conv_mixer(x2d, conv_norm_g[j][None, :], conv_w_in_b, conv_w, conv_w_out_b,
                              layer=j, seq=seq)
        x2d = _mlp(x2d, mlp_norm_g[i][None, :], mlp_w1_b, mlp_w2_b, layer=i)
    return x2d.reshape(batch, seq, d)
```

```python
import functools
import math

import jax
import jax.numpy as jnp
from jax import lax
from jax.experimental import pallas as pl
from jax.experimental.pallas import tpu as pltpu

D_MODEL = 2048
N_HEADS = 16
N_KV_HEADS = 4
HEAD_DIM = 128
GROUP = N_HEADS // N_KV_HEADS
ROPE_DIM = HEAD_DIM // 4
ROPE_THETA = 500000.0
IDX_HEADS = 16
IDX_DIM = 64
IDX_ROPE_DIM = IDX_DIM // 4
TOPK_MAX = 256
Q_W = N_HEADS * HEAD_DIM
KV_W = N_KV_HEADS * HEAD_DIM
IDX_Q_W = IDX_HEADS * IDX_DIM
MAIN_W = Q_W + 2 * KV_W + IDX_Q_W
D_FF = 4 * D_MODEL
EPS = 1e-6
LANES = 128
SUBLANES = 8
VMEM_LIMIT = 56 * 1024 * 1024

INT_MIN = -(2 ** 31)
MASKED = -1e30
MAX_BISECT = 40
BISECT_UNROLL = 2
ABOVE_MAX_REL = 2.0 ** -20
Q_SCALE_LOG2E = HEAD_DIM ** -0.5 * math.log2(math.e)

F32 = jnp.float32
BF16 = jnp.bfloat16


def _rmsnorm_rows(x, g):
    ms = jnp.mean(x * x, axis=-1, keepdims=True)
    return (x * lax.rsqrt(ms + EPS)) * g


def _rope_tables(cos, sin, period, half):
    d = lax.broadcasted_iota(jnp.int32, cos.shape, 1) & (period - 1)
    c = jnp.where(d < 2 * half, cos, 1.0)
    s_lo = jnp.where(d < half, -sin, 0.0)
    s_hi = jnp.where((d >= half) & (d < 2 * half), sin, 0.0)
    return c, s_lo, s_hi


def _rope(x, c, s_lo, s_hi, half):
    return x * c + pltpu.roll(x, LANES - half, 1) * s_lo + pltpu.roll(x, half, 1) * s_hi


V_ONES_ROWS = 2 * SUBLANES
IDX_LANE_SHIFT = ROPE_DIM


def _attn_in_kernel(x_ref, pos_ref, g_ref, w_ref, wkw_ref, qgain_ref, kgain_ref, inv_ref,
                    qg_ref, k_ref, vt_ref, qit_ref, ki_ref, wit_ref, h_ref, acc_ref, tq_ref, ti_ref,
                    *, tn, tq, kc):
    j = pl.program_id(1)
    tm = x_ref.shape[0]
    n_q = Q_W // tn
    n_tiles = MAIN_W // tn

    def project(slot):
        acc_ref[slot] = jnp.dot(h_ref[...], w_ref[...], preferred_element_type=F32)

    def prologue():
        h = _rmsnorm_rows(x_ref[...], g_ref[...]).astype(BF16)
        h_ref[...] = h
        ang = pos_ref[...].astype(F32) * inv_ref[...]
        cos, sin = jnp.cos(ang), jnp.sin(ang)
        cq, sq_lo, sq_hi = _rope_tables(cos, sin, HEAD_DIM, ROPE_DIM // 2)
        tq_ref[0], tq_ref[1], tq_ref[2] = cq, sq_lo, sq_hi
        back = LANES - IDX_LANE_SHIFT
        ci, si_lo, si_hi = _rope_tables(pltpu.roll(cos, back, 1), pltpu.roll(sin, back, 1),
                                        IDX_DIM, IDX_ROPE_DIM // 2)
        ti_ref[0], ti_ref[1], ti_ref[2] = ci, si_lo, si_hi
        kw = jnp.dot(h, wkw_ref[...], preferred_element_type=F32)
        roped = _rope(kw, ci, si_lo, si_hi, IDX_ROPE_DIM // 2)
        ki_ref[...] = roped[:, :IDX_DIM].astype(BF16)
        scale = IDX_HEADS ** -0.5 * IDX_DIM ** -0.5
        wit_ref[0] = (kw * scale).T[IDX_DIM:IDX_DIM + IDX_HEADS, :]

    def head_norm_rope(acc, hh, gain_ref):
        y = _rmsnorm_rows(acc[:, hh * HEAD_DIM:(hh + 1) * HEAD_DIM], gain_ref[...])
        return _rope(y, tq_ref[0], tq_ref[1], tq_ref[2], ROPE_DIM // 2)

    def finish(tile, slot):
        acc = acc_ref[slot]
        if tile < n_q:
            for g in range(GROUP):
                y = head_norm_rope(acc, g, qgain_ref) * Q_SCALE_LOG2E
                for r in range(tm // tq):
                    qg_ref[r, 0, :, g * tq:(g + 1) * tq] = y[r * tq:(r + 1) * tq, :].T.astype(BF16)
        elif tile == n_q:
            for hh in range(N_KV_HEADS):
                k_ref[:, hh * HEAD_DIM:(hh + 1) * HEAD_DIM] = (
                    head_norm_rope(acc, hh, kgain_ref).astype(BF16))
        elif tile == n_q + 1:
            for n in range(N_KV_HEADS):
                for cc in range(tm // kc):
                    v = acc[cc * kc:(cc + 1) * kc, n * HEAD_DIM:(n + 1) * HEAD_DIM]
                    vt_ref[cc, n, :HEAD_DIM, :] = v.T.astype(BF16)
                    vt_ref[cc, n, HEAD_DIM:, :] = jnp.ones((V_ONES_ROWS, kc), BF16)
        else:
            for hh in range(tn // LANES):
                sl = slice(hh * LANES, (hh + 1) * LANES)
                y = _rope(acc[:, sl], ti_ref[0], ti_ref[1], ti_ref[2], IDX_ROPE_DIM // 2)
                qit_ref[0, sl, :] = y.T.astype(BF16)

    def kind(tile):
        return "q" if tile < n_q else {n_q: "k", n_q + 1: "v"}.get(tile, "qi")

    bodies = {}
    for step in range(n_tiles + 1):
        sig = (step == 0, step < n_tiles, step % 2, kind(step - 1) if step > 0 else None)
        bodies.setdefault(sig, []).append(step)
    for steps in bodies.values():
        cond = functools.reduce(lambda a, b: a | b, [j == s for s in steps])

        @pl.when(cond)
        def _(step=steps[0]):
            if step == 0:
                prologue()
            if step < n_tiles:
                project(step % 2)
            if step > 0:
                finish(step - 1, (step - 1) % 2)


def _attn_in(x2d, pos2d, g, w_in, w_kw, q_g, k_g, inv, *, layer, batch, seq, tq, kc, tm=1024, tn=512):
    t = x2d.shape[0]
    assert tn == KV_W == GROUP * HEAD_DIM and tm % tq == 0 and tm % kc == 0 and seq % tm == 0
    tiles_per_seq = seq // tm
    n_q = Q_W // tn
    n_tiles = MAIN_W // tn
    first_qi = n_tiles - IDX_Q_W // tn
    return pl.pallas_call(
        functools.partial(_attn_in_kernel, tn=tn, tq=tq, kc=kc),
        grid=(t // tm, n_tiles + 1),
        in_specs=[
            pl.BlockSpec((tm, D_MODEL), lambda i, j: (i, 0)),
            pl.BlockSpec((tm, 1), lambda i, j: (i, 0)),
            pl.BlockSpec((1, D_MODEL), lambda i, j: (0, 0)),
            pl.BlockSpec((None, D_MODEL, tn), lambda i, j: (layer, 0, jnp.minimum(j, n_tiles - 1))),
            pl.BlockSpec((None, D_MODEL, LANES), lambda i, j: (layer, 0, 0)),
            pl.BlockSpec((1, HEAD_DIM), lambda i, j: (0, 0)),
            pl.BlockSpec((1, HEAD_DIM), lambda i, j: (0, 0)),
            pl.BlockSpec((1, LANES), lambda i, j: (0, 0)),
        ],
        out_specs=[
            pl.BlockSpec((tm // tq, 1, HEAD_DIM, GROUP * tq),
                         lambda i, j: (i, jnp.clip(j - 1, 0, n_q - 1), 0, 0)),
            pl.BlockSpec((tm, KV_W), lambda i, j: (i, 0)),
            pl.BlockSpec((tm // kc, N_KV_HEADS, HEAD_DIM + V_ONES_ROWS, kc), lambda i, j: (i, 0, 0, 0)),
            pl.BlockSpec((1, tn, tm),
                         lambda i, j: (i // tiles_per_seq, jnp.maximum(j - 1 - first_qi, 0),
                                       i % tiles_per_seq)),
            pl.BlockSpec((tm, IDX_DIM), lambda i, j: (i, 0)),
            pl.BlockSpec((1, IDX_HEADS, tm), lambda i, j: (i // tiles_per_seq, 0, i % tiles_per_seq)),
        ],
        out_shape=[
            jax.ShapeDtypeStruct((t // tq, N_KV_HEADS, HEAD_DIM, GROUP * tq), BF16),
            jax.ShapeDtypeStruct((t, KV_W), BF16),
            jax.ShapeDtypeStruct((t // kc, N_KV_HEADS, HEAD_DIM + V_ONES_ROWS, kc), BF16),
            jax.ShapeDtypeStruct((batch, IDX_Q_W, seq), BF16),
            jax.ShapeDtypeStruct((t, IDX_DIM), BF16),
            jax.ShapeDtypeStruct((batch, IDX_HEADS, seq), F32),
        ],
        scratch_shapes=[
            pltpu.VMEM((tm, D_MODEL), BF16),
            pltpu.VMEM((2, tm, tn), F32),
            pltpu.VMEM((3, tm, LANES), F32),
            pltpu.VMEM((3, tm, LANES), F32),
        ],
        compiler_params=pltpu.CompilerParams(
            dimension_semantics=("arbitrary", "arbitrary"), vmem_limit_bytes=VMEM_LIMIT),
        name="attn_in",
    )(x2d, pos2d, g, w_in, w_kw, q_g, k_g, inv)


def _fold8(x, op):
    rows, cols = x.shape
    x = x.reshape(rows // SUBLANES, SUBLANES, cols)
    if op is jnp.add:
        return x.sum(axis=0)
    return x.min(axis=0) if op is jnp.minimum else x.max(axis=0)


def _dsa_kernel(qit_ref, wit_ref, ki_ref, qg_ref, k_ref, vt_ref, o_ref, sc_ref, bias_ref,
                m_ref, acc_ref, *, tq, kc, top_k, idx_bits):
    i = pl.program_id(1)
    t0 = i * tq
    n_kc = (t0 + tq + kc - 1) // kc
    row = lax.broadcasted_iota(jnp.int32, (kc, tq), 0)
    t_idx = t0 + lax.broadcasted_iota(jnp.int32, (kc, tq), 1)

    def score_chunk(c, carry):
        lo8, hi8 = carry
        k_rows = ki_ref[0, pl.ds(pl.multiple_of(c * kc, kc), kc), :]
        acc = jnp.zeros((kc, tq), F32)
        for h in range(IDX_HEADS):
            rel = jnp.dot(k_rows, qit_ref[0, h * IDX_DIM:(h + 1) * IDX_DIM, :],
                          preferred_element_type=F32)
            acc = acc + wit_ref[0, h:h + 1, :] * jnp.maximum(rel, 0.0)
        causal = c * kc + row <= t_idx
        sc_ref[c] = jnp.where(causal, acc, -jnp.inf)
        lo8 = jnp.minimum(lo8, _fold8(jnp.where(causal, acc, jnp.inf), jnp.minimum))
        hi8 = jnp.maximum(hi8, _fold8(jnp.where(causal, acc, -jnp.inf), jnp.maximum))
        return lo8, hi8

    lo8, hi8 = lax.fori_loop(0, n_kc, score_chunk, (jnp.full((SUBLANES, tq), jnp.inf, F32),
                                                     jnp.full((SUBLANES, tq), -jnp.inf, F32)))
    row_min = jnp.min(lo8, axis=0, keepdims=True)
    row_max = jnp.max(hi8, axis=0, keepdims=True)

    def count(pred_fn):
        def body(c, cnt):
            return cnt + _fold8(pred_fn(c, sc_ref[c]).astype(jnp.int32), jnp.add)
        part = lax.fori_loop(0, n_kc, body, jnp.zeros((SUBLANES, tq), jnp.int32))
        return part.sum(axis=0, keepdims=True)

    n_causal = t0 + 1 + lax.broadcasted_iota(jnp.int32, (1, tq), 1)
    short = n_causal <= top_k
    done0 = short.astype(F32)
    above_max = row_max + jnp.maximum(jnp.abs(row_max) * ABOVE_MAX_REL, jnp.finfo(F32).tiny)

    def bisect_cond(st):
        it, _, _, _, done = st
        return (it < MAX_BISECT) & (jnp.min(done) == 0.0)

    def bisect_step(st):
        it, lo, hi, thr, done = st
        mid = 0.5 * lo + 0.5 * hi
        inside = (mid > lo) & (mid < hi)
        cnt = count(lambda c, sc: sc >= mid)
        active = done == 0.0
        hit = active & inside & (cnt == top_k)
        converged = active & ~inside
        thr = jnp.where(hit, mid, jnp.where(converged, lo, thr))
        done = jnp.where(hit | converged, 1.0, done)
        lo = jnp.where(active & inside & (cnt > top_k), mid, lo)
        hi = jnp.where(active & inside & (cnt < top_k), mid, hi)
        return it + 1, lo, hi, thr, done

    def bisect_body(st):
        for _ in range(BISECT_UNROLL):
            st = bisect_step(st)
        return st

    _, _, _, thr, done = lax.while_loop(bisect_cond, bisect_body,
                                        (jnp.int32(0), row_min, above_max, row_min, done0))

    def to_key(x):
        bits = pltpu.bitcast(x, jnp.int32)
        return bits ^ ((bits >> 31) & 0x7FFFFFFF)

    def exact_kth(_):
        lo = jnp.where(count(lambda c, sc: to_key(sc) >= 0) >= top_k, 0, INT_MIN).astype(jnp.int32)

        def bit_body(b, lo):
            cand = lo + jnp.left_shift(jnp.int32(1), 30 - b)
            return jnp.where(count(lambda c, sc: to_key(sc) >= cand) >= top_k, cand, lo)

        kth = lax.fori_loop(0, 31, bit_body, lo)
        kth = pltpu.bitcast(kth ^ ((kth >> 31) & 0x7FFFFFFF), F32)
        return jnp.where(done == 0.0, kth, thr)

    thr = lax.cond(jnp.min(done) == 0.0, exact_kth, lambda _: thr, 0)

    def bias_chunk(c, cnt):
        sel = sc_ref[c] >= thr
        bias_ref[c] = jnp.where(sel, 0.0, MASKED)
        return cnt + _fold8(sel.astype(jnp.int32), jnp.add)

    n_sel = lax.fori_loop(0, n_kc, bias_chunk, jnp.zeros((SUBLANES, tq), jnp.int32))
    n_sel = n_sel.sum(axis=0, keepdims=True)

    @pl.when(jnp.max(n_sel.astype(F32)) > top_k)
    def _():
        need = top_k - count(lambda c, sc: sc > thr)

        def cut_body(b, p):
            cand = p + jnp.left_shift(jnp.int32(1), idx_bits - 1 - b)
            below = count(lambda c, sc: (sc == thr) & (c * kc + row < cand))
            return jnp.where(below < need, cand, p)

        cut = lax.fori_loop(0, idx_bits, cut_body, jnp.zeros((1, tq), jnp.int32))

        def rebias_chunk(c, carry):
            sc = sc_ref[c]
            sel = (sc > thr) | ((sc == thr) & (c * kc + row <= cut))
            bias_ref[c] = jnp.where(sel, 0.0, MASKED)
            return carry

        lax.fori_loop(0, n_kc, rebias_chunk, 0)

    m_ref[...] = jnp.full(m_ref.shape, MASKED, F32)
    acc_ref[...] = jnp.zeros(acc_ref.shape, F32)

    def attn_chunk(c, carry):
        bias = bias_ref[c]
        rows = pl.ds(pl.multiple_of(c * kc, kc), kc)
        for n in range(N_KV_HEADS):
            k_c = k_ref[0, rows, n * HEAD_DIM:(n + 1) * HEAD_DIM]
            s_all = jnp.dot(k_c, qg_ref[0, n], preferred_element_type=F32)
            ps, alphas = [], []
            for g in range(GROUP):
                h = n * GROUP + g
                s = s_all[:, g * tq:(g + 1) * tq] + bias
                m_old = m_ref[h]
                m_new = jnp.maximum(m_old, jnp.max(s, axis=0, keepdims=True))
                ps.append(jnp.exp2((s - m_new).astype(BF16)))
                alphas.append(jnp.exp2(m_old - m_new))
                m_ref[h] = m_new
            p = jnp.concatenate(ps, axis=1)
            alpha = jnp.concatenate(alphas, axis=1)
            acc_ref[n] = alpha * acc_ref[n] + jnp.dot(
                vt_ref[0, c, n], p, preferred_element_type=F32)
        return carry

    lax.fori_loop(0, n_kc, attn_chunk, 0)
    for h in range(N_HEADS):
        n, g = divmod(h, GROUP)
        cols = slice(g * tq, (g + 1) * tq)
        o_t = acc_ref[n, :HEAD_DIM, cols] / acc_ref[n, HEAD_DIM:HEAD_DIM + 1, cols]
        o_ref[0, :, h * HEAD_DIM:(h + 1) * HEAD_DIM] = o_t.T.astype(BF16)


def _dsa(qit, wit, ki, qg, k, vt, *, tq, kc):
    b, s, _ = k.shape
    top_k = min(TOPK_MAX, s // 4)
    n_chunks = s // kc
    n_qb = s // tq
    v_rows = vt.shape[3]
    per_batch = pl.Buffered(1)
    return pl.pallas_call(
        functools.partial(_dsa_kernel, tq=tq, kc=kc, top_k=top_k, idx_bits=(s - 1).bit_length()),
        grid=(b, s // tq),
        in_specs=[
            pl.BlockSpec((1, IDX_Q_W, tq), lambda bb, i: (bb, 0, i)),
            pl.BlockSpec((1, IDX_HEADS, tq), lambda bb, i: (bb, 0, i)),
            pl.BlockSpec((1, s, IDX_DIM), lambda bb, i: (bb, 0, 0), pipeline_mode=per_batch),
            pl.BlockSpec((1, N_KV_HEADS, HEAD_DIM, GROUP * tq), lambda bb, i: (bb * n_qb + i, 0, 0, 0)),
            pl.BlockSpec((1, s, KV_W), lambda bb, i: (bb, 0, 0), pipeline_mode=per_batch),
            pl.BlockSpec((1, n_chunks, N_KV_HEADS, v_rows, kc), lambda bb, i: (bb, 0, 0, 0, 0),
                         pipeline_mode=per_batch),
        ],
        out_specs=pl.BlockSpec((1, tq, Q_W), lambda bb, i: (bb, i, 0)),
        out_shape=jax.ShapeDtypeStruct((b, s, Q_W), BF16),
        scratch_shapes=[
            pltpu.VMEM((n_chunks, kc, tq), F32),
            pltpu.VMEM((n_chunks, kc, tq), F32),
            pltpu.VMEM((N_HEADS, 1, tq), F32),
            pltpu.VMEM((N_KV_HEADS, v_rows, GROUP * tq), F32),
        ],
        compiler_params=pltpu.CompilerParams(
            dimension_semantics=("arbitrary", "arbitrary"), vmem_limit_bytes=VMEM_LIMIT),
        name="dsa",
    )(qit, wit, ki, qg, k, vt)


def _out_proj_kernel(a_ref, w_ref, x_ref, o_ref):
    o_ref[...] = x_ref[...] + jnp.dot(a_ref[...], w_ref[...], preferred_element_type=F32)


def _out_proj(a2d, w, x2d, *, layer, tm=512):
    t, k = a2d.shape
    n = w.shape[2]
    return pl.pallas_call(
        _out_proj_kernel,
        grid=(t // tm,),
        in_specs=[
            pl.BlockSpec((tm, k), lambda i: (i, 0)),
            pl.BlockSpec((None, k, n), lambda i: (layer, 0, 0), pipeline_mode=pl.Buffered(1)),
            pl.BlockSpec((tm, n), lambda i: (i, 0)),
        ],
        out_specs=pl.BlockSpec((tm, n), lambda i: (i, 0)),
        out_shape=jax.ShapeDtypeStruct((t, n), F32),
        compiler_params=pltpu.CompilerParams(
            dimension_semantics=("arbitrary",), vmem_limit_bytes=VMEM_LIMIT),
        name="out_proj",
    )(a2d, w, x2d)


def _conv_kernel(x_ref, g_ref, wb_ref, wc_ref, wu_ref, cw_ref, wo_ref, o_ref, h_ref, carry_ref,
                 *, tm, tiles_per_seq):
    i = pl.program_id(0)
    j = pl.program_id(1)

    def mixer_chunk():
        h = h_ref[...]
        b_gate = jnp.dot(h, wb_ref[...], preferred_element_type=F32)
        c_gate = jnp.dot(h, wc_ref[...], preferred_element_type=F32)
        u = jnp.dot(h, wu_ref[...], preferred_element_type=F32)
        z = c_gate * u
        prev = jnp.where(i % tiles_per_seq == 0, 0.0, carry_ref[j])
        carry_ref[j] = z[tm - SUBLANES:, :]
        r = lax.broadcasted_iota(jnp.int32, z.shape, 0)
        z1 = jnp.where(r == 0, prev[7:8, :], pltpu.roll(z, 1, 0))
        z2 = jnp.where(r == 0, prev[6:7, :], jnp.where(r == 1, prev[7:8, :], pltpu.roll(z, 2, 0)))
        y = z2 * cw_ref[0:1, :] + z1 * cw_ref[1:2, :] + z * cw_ref[2:3, :]
        gated = (b_gate * y).astype(BF16)
        return jnp.dot(gated, wo_ref[...], preferred_element_type=F32)

    @pl.when(j == 0)
    def _():
        x = x_ref[...]
        h_ref[...] = _rmsnorm_rows(x, g_ref[...]).astype(BF16)
        o_ref[...] = x + mixer_chunk()

    @pl.when(j > 0)
    def _():
        o_ref[...] += mixer_chunk()


def _conv_mixer(x2d, g, w_in, cw, wo, *, layer, seq, tm=512, tn=512):
    t = x2d.shape[0]
    n_chunks = D_MODEL // tn
    return pl.pallas_call(
        functools.partial(_conv_kernel, tm=tm, tiles_per_seq=seq // tm),
        grid=(t // tm, n_chunks),
        in_specs=[
            pl.BlockSpec((tm, D_MODEL), lambda i, j: (i, 0)),
            pl.BlockSpec((1, D_MODEL), lambda i, j: (0, 0)),
            pl.BlockSpec((None, D_MODEL, tn), lambda i, j: (layer, 0, j)),
            pl.BlockSpec((None, D_MODEL, tn), lambda i, j: (layer, 0, j + n_chunks)),
            pl.BlockSpec((None, D_MODEL, tn), lambda i, j: (layer, 0, j + 2 * n_chunks)),
            pl.BlockSpec((None, 3, tn), lambda i, j: (layer, 0, j)),
            pl.BlockSpec((None, tn, D_MODEL), lambda i, j: (layer, j, 0)),
        ],
        out_specs=pl.BlockSpec((tm, D_MODEL), lambda i, j: (i, 0)),
        out_shape=jax.ShapeDtypeStruct((t, D_MODEL), F32),
        scratch_shapes=[
            pltpu.VMEM((tm, D_MODEL), BF16),
            pltpu.VMEM((n_chunks, SUBLANES, tn), F32),
        ],
        compiler_params=pltpu.CompilerParams(
            dimension_semantics=("arbitrary", "arbitrary"), vmem_limit_bytes=VMEM_LIMIT),
        name="conv_mixer",
    )(x2d, g, w_in, w_in, w_in, cw, wo)


def _mlp_kernel(x_ref, g_ref, w1_ref, w2_ref, o_ref, h_ref):
    j = pl.program_id(1)

    def hidden_chunk():
        a = jnp.maximum(jnp.dot(h_ref[...], w1_ref[...], preferred_element_type=F32), 0.0)
        return jnp.dot((a * a).astype(BF16), w2_ref[...], preferred_element_type=F32)

    @pl.when(j == 0)
    def _():
        x = x_ref[...]
        h_ref[...] = _rmsnorm_rows(x, g_ref[...]).astype(BF16)
        o_ref[...] = x + hidden_chunk()

    @pl.when(j > 0)
    def _():
        o_ref[...] += hidden_chunk()


def _mlp(x2d, g, w1, w2, *, layer, tm=1024, tf=512):
    t = x2d.shape[0]
    return pl.pallas_call(
        _mlp_kernel,
        grid=(t // tm, D_FF // tf),
        in_specs=[
            pl.BlockSpec((tm, D_MODEL), lambda i, j: (i, 0)),
            pl.BlockSpec((1, D_MODEL), lambda i, j: (0, 0)),
            pl.BlockSpec((None, D_MODEL, tf), lambda i, j: (layer, 0, j)),
            pl.BlockSpec((None, tf, D_MODEL), lambda i, j: (layer, j, 0)),
        ],
        out_specs=pl.BlockSpec((tm, D_MODEL), lambda i, j: (i, 0)),
        out_shape=jax.ShapeDtypeStruct((t, D_MODEL), F32),
        scratch_shapes=[pltpu.VMEM((tm, D_MODEL), BF16)],
        compiler_params=pltpu.CompilerParams(
            dimension_semantics=("arbitrary", "arbitrary"), vmem_limit_bytes=VMEM_LIMIT),
        name="mlp",
    )(x2d, g, w1, w2)


def _rope_inv_lanes():
    def inv(rot_dim):
        return ROPE_THETA ** (-jnp.arange(0, rot_dim, 2, dtype=F32) / rot_dim)
    inv_q, inv_i = jnp.tile(inv(ROPE_DIM), 2), jnp.tile(inv(IDX_ROPE_DIM), 2)
    gap = jnp.zeros((IDX_DIM - IDX_ROPE_DIM,), F32)
    tail = jnp.zeros((LANES - IDX_LANE_SHIFT - IDX_DIM - IDX_ROPE_DIM,), F32)
    lanes = jnp.concatenate([inv_q, inv_i, gap, inv_i, tail])
    assert lanes.shape == (LANES,) and IDX_LANE_SHIFT == inv_q.shape[0]
    return lanes[None, :]


def _dsa_layer(x2d, pos2d, batch, seq, g, w_in, w_kw, q_g, k_g, w_out, layer, *, tq=512, kc=512):
    qg, k, vt, qit, ki, wit = _attn_in(
        x2d, pos2d, g[None, :], w_in, w_kw, q_g[None, :], k_g[None, :], _rope_inv_lanes(),
        layer=layer, batch=batch, seq=seq, tq=tq, kc=kc)
    o = _dsa(qit, wit, ki.reshape(batch, seq, IDX_DIM), qg, k.reshape(batch, seq, KV_W),
             vt.reshape((batch, seq // kc) + vt.shape[1:]), tq=tq, kc=kc)
    return _out_proj(o.reshape(batch * seq, Q_W), w_out, x2d, layer=layer)


def kernel(x, positions, attn_norm_g, attn_w_in, attn_q_norm_g, attn_k_norm_g, attn_w_out,
           conv_norm_g, conv_w_in, conv_w, conv_w_out, mlp_norm_g, mlp_w1, mlp_w2):
    batch, seq, d = x.shape
    depth = mlp_w1.shape[0]
    x2d = x.reshape(batch * seq, d)
    pos2d = positions.reshape(batch * seq, 1)
    attn_w_in_b, attn_w_out_b = attn_w_in[:, :, :MAIN_W].astype(BF16), attn_w_out.astype(BF16)
    attn_w_kw_b = jnp.pad(attn_w_in[:, :, MAIN_W:],
                          ((0, 0), (0, 0), (0, LANES - (IDX_DIM + IDX_HEADS)))).astype(BF16)
    conv_w_in_b, conv_w_out_b = conv_w_in.astype(BF16), conv_w_out.astype(BF16)
    mlp_w1_b, mlp_w2_b = mlp_w1.astype(BF16), mlp_w2.astype(BF16)
    for i in range(depth):
        j = i // 2
        if i % 2 == 0:
            x2d = _dsa_layer(x2d, pos2d, batch, seq, attn_norm_g[j], attn_w_in_b, attn_w_kw_b,
                             attn_q_norm_g[j], attn_k_norm_g[j], attn_w_out_b, j)
        else:
            x2d = _conv_mixer(x2d, conv_norm_g[j][None, :], conv_w_in_b, conv_w, conv_w_out_b,
                              layer=j, seq=seq)
        x2d = _mlp(x2d, mlp_norm_g[i][None, :], mlp_w1_b, mlp_w2_b, layer=i)
    return x2d.reshape(batch, seq, d)
```
